```python
import numpy as np
import jax, jax.numpy as jnp
from jax import lax

D_MODEL = 2048
BATCH = 8
SEQ = 4096
DEPTH = 4

ATTN_HEADS = 16
ATTN_KV_HEADS = 4
ATTN_GROUP = ATTN_HEADS // ATTN_KV_HEADS
ATTN_HEAD_DIM = 128
WINDOW = 128
ATTN_BLOCK = 128
ROPE_THETA = 500000.0
ROT_DIM = ATTN_HEAD_DIM // 4
DN_HEADS = 16
DN_KEY_DIM = 128
DN_VALUE_DIM = 128
DN_CONV = 4
DN_CHUNK = 64
D_FF = 11 * D_MODEL // 4
FFN_CONV = 3
EPS = 1e-6

ATTN_Q_W = ATTN_HEADS * ATTN_HEAD_DIM
ATTN_KV_W = ATTN_KV_HEADS * ATTN_HEAD_DIM
DN_K_W = DN_HEADS * DN_KEY_DIM
DN_V_W = DN_HEADS * DN_VALUE_DIM
DN_QKV_W = 2 * DN_K_W + DN_V_W
IN_SPLITS = (ATTN_Q_W, ATTN_KV_W, ATTN_KV_W, DN_QKV_W, DN_HEADS, DN_HEADS, DN_V_W, D_MODEL, D_MODEL)
IN_WIDTH = sum(IN_SPLITS)

kernel_name = "hybrid_swa_sink_gdn_convglu_sandwich"


def rms_norm(x, w):
    xf = x.astype(jnp.float32)
    y = xf * lax.rsqrt(jnp.mean(xf * xf, axis=-1, keepdims=True) + EPS)
    return (y * w.astype(jnp.float32)).astype(x.dtype)


def l2_norm(x):
    return x * lax.rsqrt(jnp.sum(x * x, axis=-1, keepdims=True) + EPS)


def causal_depthwise_conv(x, w):
    k_width = w.shape[0]
    s = x.shape[1]
    xp = jnp.pad(x, ((0, 0), (k_width - 1, 0), (0, 0)))
    y = xp[:, 0:s] * w[0]
    for j in range(1, k_width):
        y = y + xp[:, j:j + s] * w[j]
    return y


def rope_tables(positions):
    inv_freq = ROPE_THETA ** (-jnp.arange(0, ROT_DIM, 2, dtype=jnp.float32) / ROT_DIM)
    ang = positions.astype(jnp.float32)[..., None] * inv_freq
    return jnp.cos(ang)[:, :, None, :], jnp.sin(ang)[:, :, None, :]


def partial_rope(x, cos, sin):
    half = ROT_DIM // 2
    x1 = x[..., :half].astype(jnp.float32)
    x2 = x[..., half:ROT_DIM].astype(jnp.float32)
    r1 = (x1 * cos - x2 * sin).astype(x.dtype)
    r2 = (x2 * cos + x1 * sin).astype(x.dtype)
    return jnp.concatenate([r1, r2, x[..., ROT_DIM:]], axis=-1)


def sliding_window_attention(q, k, v, sinks):
    b, s = q.shape[0], q.shape[1]
    nb = s // ATTN_BLOCK
    qb = q.reshape(b, nb, ATTN_BLOCK, ATTN_KV_HEADS, ATTN_GROUP, ATTN_HEAD_DIM).transpose(1, 0, 2, 3, 4, 5)

    def band(t):
        tb = t.reshape(b, nb, ATTN_BLOCK, ATTN_KV_HEADS, ATTN_HEAD_DIM)
        prev = jnp.concatenate([jnp.zeros_like(tb[:, :1]), tb[:, :-1]], axis=1)
        return jnp.concatenate([prev, tb], axis=2).transpose(1, 0, 2, 3, 4)

    kb, vb = band(k), band(v)
    r = jnp.arange(ATTN_BLOCK)[:, None]
    c = jnp.arange(2 * ATTN_BLOCK)[None, :]
    dist = r + ATTN_BLOCK - c
    in_band = (dist >= 0) & (dist < WINDOW)
    sink = sinks.astype(jnp.float32).reshape(ATTN_KV_HEADS, ATTN_GROUP)[None, :, :, None, None]
    scale = ATTN_HEAD_DIM ** -0.5

    def one_block(args):
        qi, ki, vi, bi = args
        logits = jnp.einsum('bqkgd,bckd->bkgqc', qi, ki).astype(jnp.float32) * scale
        valid = in_band & ((bi * ATTN_BLOCK - ATTN_BLOCK + c) >= 0)
        logits = jnp.where(valid, logits, -jnp.inf)
        m = jnp.maximum(jnp.max(logits, axis=-1, keepdims=True), sink)
        p = jnp.exp(logits - m)
        denom = jnp.sum(p, axis=-1, keepdims=True) + jnp.exp(sink - m)
        return jnp.einsum('bkgqc,bckd->bqkgd', (p / denom).astype(vi.dtype), vi)

    out = lax.map(one_block, (qb, kb, vb, jnp.arange(nb)))
    return out.transpose(1, 0, 2, 3, 4, 5).reshape(b, s, ATTN_Q_W)


def gated_delta_rule(q, k, v, beta, g):
    b, s, h, _ = q.shape
    nc = s // DN_CHUNK
    q = l2_norm(q) * (DN_KEY_DIM ** -0.5)
    k = l2_norm(k)

    def chunks(t):
        return t.reshape(b, nc, DN_CHUNK, h, t.shape[-1]).transpose(0, 1, 3, 2, 4)

    q, k, v = chunks(q), chunks(k), chunks(v)
    beta = beta.reshape(b, nc, DN_CHUNK, h).transpose(0, 1, 3, 2)
    g = jnp.cumsum(g.reshape(b, nc, DN_CHUNK, h).transpose(0, 1, 3, 2), axis=-1)
    tri = jnp.tril(jnp.ones((DN_CHUNK, DN_CHUNK), dtype=bool))
    strict = jnp.tril(jnp.ones((DN_CHUNK, DN_CHUNK), dtype=bool), -1)
    decay = jnp.exp(jnp.where(tri, g[..., :, None] - g[..., None, :], -jnp.inf))
    kk = jnp.einsum('bnhid,bnhjd->bnhij', k, k)
    a_mat = jnp.where(strict, beta[..., :, None] * kk * decay, 0.0)
    eye = jnp.eye(DN_CHUNK, dtype=jnp.float32)
    t_inv = lax.linalg.triangular_solve(eye + a_mat, jnp.broadcast_to(eye, a_mat.shape),
                                        left_side=True, lower=True, unit_diagonal=True)
    u = jnp.einsum('bnhij,bnhjd->bnhid', t_inv, v * beta[..., None])
    w = jnp.einsum('bnhij,bnhjd->bnhid', t_inv, k * (beta * jnp.exp(g))[..., None])
    qk = jnp.where(tri, jnp.einsum('bnhid,bnhjd->bnhij', q, k) * decay, 0.0)

    def step(state, inp):
        qc, kc, uc, wc, gc, qkc = inp
        v_new = uc - jnp.einsum('bhcd,bhde->bhce', wc, state)
        o = (jnp.einsum('bhcd,bhde->bhce', qc * jnp.exp(gc)[..., None], state)
             + jnp.einsum('bhij,bhje->bhie', qkc, v_new))
        g_last = gc[..., -1]
        state = (state * jnp.exp(g_last)[..., None, None]
                 + jnp.einsum('bhcd,bhce->bhde', kc * jnp.exp(g_last[..., None] - gc)[..., None], v_new))
        return state, o

    xs = tuple(jnp.moveaxis(t, 1, 0) for t in (q, k, u, w, g, qk))
    state0 = jnp.zeros((b, h, DN_KEY_DIM, DN_VALUE_DIM), jnp.float32)
    _, o = lax.scan(step, state0, xs)
    return o.transpose(1, 0, 3, 2, 4).reshape(b, s, h, DN_VALUE_DIM)


def hybrid_mixer(xn, cos, sin, w_in, sinks, dn_conv_w, dn_a_log, dn_dt_bias, dn_norm_w,
                 w_branch_attn, w_branch_dn, w_out):
    b, s, _ = xn.shape
    proj = xn @ w_in
    idx = np.cumsum(IN_SPLITS)[:-1].tolist()
    q_a, k_a, v_a, qkv_d, b_d, a_d, z_d, g_a, g_d = jnp.split(proj, idx, axis=-1)

    q_a = partial_rope(q_a.reshape(b, s, ATTN_HEADS, ATTN_HEAD_DIM), cos, sin)
    k_a = partial_rope(k_a.reshape(b, s, ATTN_KV_HEADS, ATTN_HEAD_DIM), cos, sin)
    v_a = v_a.reshape(b, s, ATTN_KV_HEADS, ATTN_HEAD_DIM)
    o_a = sliding_window_attention(q_a, k_a, v_a, sinks)

    qkv_d = jax.nn.silu(causal_depthwise_conv(qkv_d, dn_conv_w)).astype(jnp.float32)
    q_d, k_d, v_d = jnp.split(qkv_d, [DN_K_W, 2 * DN_K_W], axis=-1)
    beta = jax.nn.sigmoid(b_d.astype(jnp.float32))
    g = -jnp.exp(dn_a_log.astype(jnp.float32)) * jax.nn.softplus(b_d.dtype.type(1) * a_d.astype(jnp.float32) + dn_dt_bias.astype(jnp.float32))
    o_d = gated_delta_rule(q_d.reshape(b, s, DN_HEADS, DN_KEY_DIM), k_d.reshape(b, s, DN_HEADS, DN_KEY_DIM),
                           v_d.reshape(b, s, DN_HEADS, DN_VALUE_DIM), beta, g)
    z = z_d.reshape(b, s, DN_HEADS, DN_VALUE_DIM).astype(jnp.float32)
    o_d = (o_d * lax.rsqrt(jnp.mean(o_d * o_d, axis=-1, keepdims=True) + EPS)
           * dn_norm_w.astype(jnp.float32) * jax.nn.silu(z))
    o_d = o_d.reshape(b, s, DN_V_W).astype(xn.dtype)

    y = jax.nn.sigmoid(g_a) * (o_a @ w_branch_attn) + jax.nn.sigmoid(g_d) * (o_d @ w_branch_dn)
    return y @ w_out


def conv_glu_ffn(xn, w_up, conv_w, conv_b, w_down):
    u = causal_depthwise_conv(xn @ w_up, conv_w) + conv_b
    gate, val = jnp.split(u, 2, axis=-1)
    return (jax.nn.silu(gate) * val) @ w_down


def setup_inputs(seed: int = 0) -> dict:
    key = jax.random.key(seed)
    ks = jax.random.split(key, 20)
    f32 = jnp.float32
    L, D = DEPTH, D_MODEL

    def normal(k, shape, scale):
        return jax.random.normal(k, shape, f32) * scale

    def gain(k, shape):
        return 1.0 + 0.02 * jax.random.normal(k, shape, f32)

    x = normal(ks[0], (BATCH, SEQ, D), 1.0)
    positions = (jax.random.randint(ks[1], (BATCH, 1), 0, 4096, dtype=jnp.int32)
                 + jnp.arange(SEQ, dtype=jnp.int32)[None, :])
    dt = jnp.exp(jax.random.uniform(ks[7], (L, DN_HEADS), f32, np.log(1e-3), np.log(1e-1)))
    return {
        "x": x,
        "positions": positions,
        "norm_mix_pre": gain(ks[2], (L, D)),
        "w_in": normal(ks[3], (L, D, IN_WIDTH), D ** -0.5),
        "attn_sinks": normal(ks[4], (L, ATTN_HEADS), 1.0),
        "dn_conv_w": normal(ks[5], (L, DN_CONV, DN_QKV_W), DN_CONV ** -0.5),
        "dn_a_log": jnp.log(jax.random.uniform(ks[6], (L, DN_HEADS), f32, 1.0, 16.0)),
        "dn_dt_bias": dt + jnp.log(-jnp.expm1(-dt)),
        "dn_norm_w": gain(ks[8], (L, DN_VALUE_DIM)),
        "w_branch_attn": normal(ks[9], (L, ATTN_Q_W, D), ATTN_Q_W ** -0.5),
        "w_branch_dn": normal(ks[10], (L, DN_V_W, D), DN_V_W ** -0.5),
        "w_out": normal(ks[11], (L, D, D), D ** -0.5),
        "norm_mix_post": gain(ks[12], (L, D)),
        "norm_ffn_pre": gain(ks[13], (L, D)),
        "w_ffn_up": normal(ks[14], (L, D, 2 * D_FF), D ** -0.5),
        "ffn_conv_w": normal(ks[15], (L, FFN_CONV, 2 * D_FF), FFN_CONV ** -0.5),
        "ffn_conv_b": normal(ks[16], (L, 2 * D_FF), 0.01),
        "w_ffn_down": normal(ks[17], (L, D_FF, D), D_FF ** -0.5),
        "norm_ffn_post": gain(ks[18], (L, D)),
    }


def reference(x, positions, norm_mix_pre, w_in, attn_sinks, dn_conv_w, dn_a_log, dn_dt_bias, dn_norm_w,
              w_branch_attn, w_branch_dn, w_out, norm_mix_post, norm_ffn_pre, w_ffn_up, ffn_conv_w,
              ffn_conv_b, w_ffn_down, norm_ffn_post):
    cos, sin = rope_tables(positions)
    h = x
    for l in range(DEPTH):
        xn = rms_norm(h, norm_mix_pre[l])
        mix = hybrid_mixer(xn, cos, sin, w_in[l], attn_sinks[l], dn_conv_w[l], dn_a_log[l], dn_dt_bias[l],
                           dn_norm_w[l], w_branch_attn[l], w_branch_dn[l], w_out[l])
        h = h + rms_norm(mix, norm_mix_post[l])
        xf = rms_norm(h, norm_ffn_pre[l])
        f = conv_glu_ffn(xf, w_ffn_up[l], ffn_conv_w[l], ffn_conv_b[l], w_ffn_down[l])
        h = h + rms_norm(f, norm_ffn_post[l])
    return h
```

```python
import functools

import jax
import jax.numpy as jnp
from jax import lax
from jax.experimental import pallas as pl
from jax.experimental.pallas import tpu as pltpu

HEAD_DIM = 128
ATTN_KV_HEADS = 4
ATTN_BLOCK = 128
ROT_DIM = HEAD_DIM // 4
ROPE_THETA = 500000.0
DN_CHUNK = 64
DN_CONV = 4
FFN_CONV = 3
EPS = 1e-6

V7X_LANES = 128
V7X_SUBLANES = 8
V7X_VMEM_BYTES = 64 * 1024 * 1024

INV_PASSES = 3
INV_BASE = 16

F32 = jnp.float32
BF16 = jnp.bfloat16


def _dot(a, b):
    return jnp.dot(a, b, preferred_element_type=F32)


def _dot_nt(a, b):
    return lax.dot_general(a, b, (((1,), (1,)), ((), ())), preferred_element_type=F32)


def _dot_tn(a, b):
    return lax.dot_general(a, b, (((0,), (0,)), ((), ())), preferred_element_type=F32)


def _split_dot(a, b, passes):
    ah = a.astype(BF16)
    bh = b.astype(BF16)
    out = _dot(ah, bh)
    if passes >= 3:
        al = (a - ah.astype(F32)).astype(BF16)
        bl = (b - bh.astype(F32)).astype(BF16)
        out = out + _dot(ah, bl) + _dot(al, bh)
    return out


def _rms(x, w):
    return x * lax.rsqrt(jnp.mean(x * x, axis=-1, keepdims=True) + EPS) * w


def _silu(x):
    return x * jax.nn.sigmoid(x)


def _vmem_limit(block_bytes):
    return int(min(V7X_VMEM_BYTES - (2 << 20), 2 * block_bytes + (12 << 20)))


def _params(semantics, block_bytes):
    return pltpu.CompilerParams(dimension_semantics=semantics,
                                vmem_limit_bytes=_vmem_limit(block_bytes))


def _tile(n, want, align=V7X_LANES):
    for t in range(min(n, want) // align * align, 0, -align):
        if n % t == 0:
            return t
    raise ValueError(f"no {align}-aligned tile divides {n}")


def _rmsnorm_kernel(x_ref, w_ref, o_ref):
    o_ref[...] = _rms(x_ref[...], w_ref[...]).astype(o_ref.dtype)


def rmsnorm(x, w):
    t, d = x.shape
    tm = _tile(t, 512)
    return pl.pallas_call(
        _rmsnorm_kernel,
        grid=(t // tm,),
        in_specs=[pl.BlockSpec((tm, d), lambda i: (i, 0)),
                  pl.BlockSpec((1, d), lambda i: (0, 0))],
        out_specs=pl.BlockSpec((tm, d), lambda i: (i, 0)),
        out_shape=jax.ShapeDtypeStruct((t, d), BF16),
        compiler_params=_params(("parallel",), tm * d * 6),
        name="rmsnorm",
    )(x, w.reshape(1, d))


def _matmul_kernel(x_ref, w_ref, o_ref):
    o_ref[...] = _dot(x_ref[...], w_ref[...]).astype(o_ref.dtype)


def matmul(x, w):
    t, k = x.shape
    n = w.shape[1]
    tm = _tile(t, 1024)
    tn = _tile(n, 1024)
    return pl.pallas_call(
        _matmul_kernel,
        grid=(t // tm, n // tn),
        in_specs=[pl.BlockSpec((tm, k), lambda i, j: (i, 0)),
                  pl.BlockSpec((k, tn), lambda i, j: (0, j))],
        out_specs=pl.BlockSpec((tm, tn), lambda i, j: (i, j)),
        out_shape=jax.ShapeDtypeStruct((t, n), BF16),
        compiler_params=_params(("parallel", "parallel"), 2 * (tm * k + k * tn + tm * tn)),
        name="in_proj",
    )(x, w)


def _rope_table_kernel(pos_ref, freq_ref, sign_ref, cos_ref, sin_ref):
    ang = pos_ref[...] * freq_ref[...]
    cos_ref[...] = jnp.cos(ang)
    sin_ref[...] = jnp.sin(ang) * sign_ref[...]


def rope_tables(positions):
    t = positions.size
    half = ROT_DIM // 2
    inv_freq = ROPE_THETA ** (-jnp.arange(0, ROT_DIM, 2, dtype=F32) / ROT_DIM)
    freq = jnp.concatenate([inv_freq, inv_freq, jnp.zeros((HEAD_DIM - ROT_DIM,), F32)]).reshape(1, HEAD_DIM)
    sign = jnp.concatenate([-jnp.ones((half,), F32), jnp.ones((HEAD_DIM - half,), F32)]).reshape(1, HEAD_DIM)
    pos = positions.astype(F32).reshape(t, 1)
    tm = _tile(t, 1024)
    row = pl.BlockSpec((1, HEAD_DIM), lambda i: (0, 0))
    out = pl.BlockSpec((tm, HEAD_DIM), lambda i: (i, 0))
    return pl.pallas_call(
        _rope_table_kernel,
        grid=(t // tm,),
        in_specs=[pl.BlockSpec((tm, 1), lambda i: (i, 0)), row, row],
        out_specs=[out, out],
        out_shape=[jax.ShapeDtypeStruct((t, HEAD_DIM), F32)] * 2,
        compiler_params=_params(("parallel",), 3 * tm * HEAD_DIM * 4),
        name="rope_tables",
    )(pos, freq, sign)


def _rotate_half_matrix():
    half = ROT_DIM // 2
    r = jnp.arange(HEAD_DIM)[:, None]
    c = jnp.arange(HEAD_DIM)[None, :]
    p = ((c < half) & (r == c + half)) | ((c >= half) & (c < ROT_DIM) & (r == c - half))
    return p.astype(BF16)


def _attn_kernel(sink_ref, q_ref, k_ref, v_ref, cos_ref, sin_ref, perm_ref, o_ref, kbuf, vbuf,
                 *, tq, n_kv, group):
    i = pl.program_id(1)
    blk = ATTN_BLOCK
    hd = HEAD_DIM

    @pl.when(i == 0)
    def _():
        kbuf[0:blk, :] = jnp.zeros((blk, n_kv * hd), kbuf.dtype)
        vbuf[0:blk, :] = jnp.zeros((blk, n_kv * hd), vbuf.dtype)

    @pl.when(i > 0)
    def _():
        kbuf[0:blk, :] = kbuf[tq:tq + blk, :]
        vbuf[0:blk, :] = vbuf[tq:tq + blk, :]

    perm = perm_ref[...]

    def rope(xb, c, s):
        return (xb.astype(F32) * c + _dot(xb, perm) * s).astype(BF16)

    cos = cos_ref[...]
    sin = sin_ref[...]
    for kh in range(n_kv):
        cols = slice(kh * hd, (kh + 1) * hd)
        kbuf[blk:blk + tq, cols] = rope(k_ref[:, cols], cos, sin)
    vbuf[blk:blk + tq, :] = v_ref[...]

    r = lax.broadcasted_iota(jnp.int32, (group * blk, 2 * blk), 0) % blk
    c = lax.broadcasted_iota(jnp.int32, (group * blk, 2 * blk), 1)
    band = (c > r) & (c <= r + blk)
    band_first = band & (c >= blk)
    scale = hd ** -0.5

    for j in range(tq // blk):
        rows = slice(j * blk, (j + 1) * blk)
        cj = cos[rows]
        sj = sin[rows]
        mask = band if j > 0 else (band_first | (band & (i > 0)))
        for kh in range(n_kv):
            cols = slice(kh * hd, (kh + 1) * hd)
            heads = [kh * group + g for g in range(group)]
            qs = jnp.concatenate(
                [rope(q_ref[rows, h * hd:(h + 1) * hd], cj, sj) for h in heads], axis=0)
            kk = kbuf[j * blk:(j + 2) * blk, cols]
            vv = vbuf[j * blk:(j + 2) * blk, cols]
            logits = jnp.where(mask, _dot_nt(qs, kk) * scale, -jnp.inf)
            sink = jnp.concatenate([jnp.full((blk, 1), sink_ref[h], F32) for h in heads], axis=0)
            m = jnp.maximum(jnp.max(logits, axis=-1, keepdims=True), sink)
            p = jnp.exp(logits - m)
            denom = jnp.sum(p, axis=-1, keepdims=True) + jnp.exp(sink - m)
            o = _dot(p.astype(BF16), vv) / denom
            for g, h in enumerate(heads):
                o_ref[rows, h * hd:(h + 1) * hd] = o[g * blk:(g + 1) * blk].astype(o_ref.dtype)


def attention(proj, cos, sin, sinks, *, seq, q_off, k_off, v_off, n_heads):
    b = proj.shape[0]
    n_kv = ATTN_KV_HEADS
    group = n_heads // n_kv
    qw = n_heads * HEAD_DIM
    kw = n_kv * HEAD_DIM
    tq = _tile(seq, 512)
    assert q_off % qw == 0 and k_off % kw == 0 and v_off % kw == 0
    kern = functools.partial(_attn_kernel, tq=tq, n_kv=n_kv, group=group)
    block_bytes = tq * (2 * qw + 2 * kw) * 2 + 2 * tq * HEAD_DIM * 4 + 2 * (tq + ATTN_BLOCK) * kw * 2
    return pl.pallas_call(
        kern,
        grid=(b, seq // tq),
        in_specs=[pl.BlockSpec(memory_space=pltpu.SMEM),
                  pl.BlockSpec((None, tq, qw), lambda bi, i: (bi, i, q_off // qw)),
                  pl.BlockSpec((None, tq, kw), lambda bi, i: (bi, i, k_off // kw)),
                  pl.BlockSpec((None, tq, kw), lambda bi, i: (bi, i, v_off // kw)),
                  pl.BlockSpec((None, tq, HEAD_DIM), lambda bi, i: (bi, i, 0)),
                  pl.BlockSpec((None, tq, HEAD_DIM), lambda bi, i: (bi, i, 0)),
                  pl.BlockSpec((HEAD_DIM, HEAD_DIM), lambda bi, i: (0, 0))],
        out_specs=pl.BlockSpec((None, tq, qw), lambda bi, i: (bi, i, 0)),
        out_shape=jax.ShapeDtypeStruct((b, seq, qw), BF16),
        scratch_shapes=[pltpu.VMEM((tq + ATTN_BLOCK, kw), BF16),
                        pltpu.VMEM((tq + ATTN_BLOCK, kw), BF16)],
        compiler_params=_params(("parallel", "arbitrary"), block_bytes),
        name="swa_attention",
    )(sinks, proj, proj, proj, cos, sin, _rotate_half_matrix())


def _gates_kernel(x_ref, w_ref, alog_ref, dt_ref, gt_ref, gr_ref, *, n_heads, chunk):
    tm = x_ref.shape[0]
    val = _dot(x_ref[...], w_ref[...])
    lane = lax.broadcasted_iota(jnp.int32, val.shape, 1)
    beta = jax.nn.sigmoid(val)
    z = val + dt_ref[...]
    softplus = jnp.maximum(z, 0.0) + jnp.log1p(jnp.exp(-jnp.abs(z)))
    g = jnp.where((lane >= n_heads) & (lane < 2 * n_heads), -jnp.exp(alog_ref[...]) * softplus, 0.0)
    r = lax.broadcasted_iota(jnp.int32, (tm, tm), 0)
    c = lax.broadcasted_iota(jnp.int32, (tm, tm), 1)
    tri = ((r // chunk == c // chunk) & (c <= r)).astype(BF16)
    hi = g.astype(BF16)
    rem = g - hi.astype(F32)
    mid = rem.astype(BF16)
    lo = (rem - mid.astype(F32)).astype(BF16)
    gc = _dot(tri, hi) + _dot(tri, mid) + _dot(tri, lo)
    gt_ref[...] = jnp.where(lane < n_heads, beta, gc)
    gct = gc.T
    for ci in range(tm // chunk):
        gr_ref[ci] = gct[n_heads:2 * n_heads, ci * chunk:(ci + 1) * chunk]


def dn_gates(xn, w_gate, a_log, dt_bias, *, n_heads):
    t, d = xn.shape
    tm = _tile(t, 256)
    pad = V7X_LANES - 2 * n_heads
    alog = jnp.concatenate([jnp.zeros((n_heads,), F32), a_log, jnp.zeros((pad,), F32)]).reshape(1, V7X_LANES)
    dt = jnp.concatenate([jnp.zeros((n_heads,), F32), dt_bias, jnp.zeros((pad,), F32)]).reshape(1, V7X_LANES)
    row = pl.BlockSpec((1, V7X_LANES), lambda i: (0, 0))
    kern = functools.partial(_gates_kernel, n_heads=n_heads, chunk=DN_CHUNK)
    return pl.pallas_call(
        kern,
        grid=(t // tm,),
        in_specs=[pl.BlockSpec((tm, d), lambda i: (i, 0)),
                  pl.BlockSpec((d, V7X_LANES), lambda i: (0, 0)), row, row],
        out_specs=[pl.BlockSpec((tm, V7X_LANES), lambda i: (i, 0)),
                   pl.BlockSpec((tm // DN_CHUNK, n_heads, DN_CHUNK), lambda i: (i, 0, 0))],
        out_shape=[jax.ShapeDtypeStruct((t, V7X_LANES), F32),
                   jax.ShapeDtypeStruct((t // DN_CHUNK, n_heads, DN_CHUNK), F32)],
        compiler_params=_params(("parallel",), tm * d * 2 + d * V7X_LANES * 2 + 2 * tm * V7X_LANES * 4),
        name="dn_gates",
    )(xn, w_gate, alog, dt)


def _unit_lower_inverse(a, passes):
    n = a.shape[0]
    r = lax.broadcasted_iota(jnp.int32, (n, n), 0)
    c = lax.broadcasted_iota(jnp.int32, (n, n), 1)
    eye = (r == c).astype(F32)
    size = INV_BASE
    in_blk = (r // size) == (c // size)
    x = jnp.where(in_blk, -a, 0.0)
    inv = eye + x
    power = 1
    while 2 * power < size:
        x = _split_dot(x, x, passes)
        inv = inv + _split_dot(inv, x, passes)
        power *= 2
    while size < n:
        merged = (r // (2 * size)) == (c // (2 * size))
        off = jnp.where(merged & ~in_blk, a, 0.0)
        inv = inv - _split_dot(inv, _split_dot(off, inv, passes), passes)
        in_blk = merged
        size *= 2
    return inv


def _dn_kernel(q_ref, k_ref, v_ref, z_ref, wq_ref, wk_ref, wv_ref, gt_ref, gr_ref, nw_ref, o_ref,
               qbuf, kbuf, vbuf, qn, kn, vn, state, *, ts, hg, chunk):
    s = pl.program_id(2)
    gi = pl.program_id(1)
    hd = HEAD_DIM
    halo = V7X_SUBLANES

    @pl.when(s == 0)
    def _():
        state[...] = jnp.zeros(state.shape, F32)
        for buf in (qbuf, kbuf, vbuf):
            buf[0:halo, :] = jnp.zeros((halo, hg * hd), F32)

    @pl.when(s > 0)
    def _():
        for buf in (qbuf, kbuf, vbuf):
            buf[0:halo, :] = buf[ts:ts + halo, :]

    def conv_silu(src_ref, buf, w_ref):
        buf[halo:halo + ts, :] = src_ref[...].astype(F32)
        w = w_ref[...]
        y = buf[halo:halo + ts, :] * w[DN_CONV - 1:DN_CONV, :]
        for j in range(DN_CONV - 1):
            lag = DN_CONV - 1 - j
            y = y + buf[halo - lag:halo - lag + ts, :] * w[j:j + 1, :]
        return _silu(y)

    def l2norm(x):
        return x * lax.rsqrt(jnp.sum(x * x, axis=-1, keepdims=True) + EPS)

    qc = conv_silu(q_ref, qbuf, wq_ref)
    kc = conv_silu(k_ref, kbuf, wk_ref)
    vn[...] = conv_silu(v_ref, vbuf, wv_ref)
    for hh in range(hg):
        cols = slice(hh * hd, (hh + 1) * hd)
        qn[:, cols] = l2norm(qc[:, cols]) * (hd ** -0.5)
        kn[:, cols] = l2norm(kc[:, cols])

    ri = lax.broadcasted_iota(jnp.int32, (chunk, chunk), 0)
    ci = lax.broadcasted_iota(jnp.int32, (chunk, chunk), 1)
    tri = ri >= ci
    strict = ri > ci
    lane = lax.broadcasted_iota(jnp.int32, (chunk, V7X_LANES), 1)
    n_heads = gr_ref.shape[1]
    nw = nw_ref[...]

    def chunk_step(cidx, carry):
        rows = pl.ds(pl.multiple_of(cidx * chunk, chunk), chunk)
        gt = gt_ref[rows, :]
        for hh in range(hg):
            cols = slice(hh * hd, (hh + 1) * hd)
            h = gi * hg + hh
            beta = jnp.sum(jnp.where(lane == h, gt, 0.0), axis=-1, keepdims=True)
            gc = jnp.sum(jnp.where(lane == n_heads + h, gt, 0.0), axis=-1, keepdims=True)
            gc_row = gr_ref[cidx, pl.ds(h, 1), :]
            g_last = gc_row[:, chunk - 1:chunk]
            q = qn[rows, cols]
            k = kn[rows, cols]
            v = vn[rows, cols]
            kb = k.astype(BF16)
            sims = _dot_nt(jnp.concatenate([kb, q.astype(BF16)], axis=0), kb)
            decay = jnp.exp(jnp.where(tri, gc - gc_row, -jnp.inf))
            a_mat = jnp.where(strict, beta * sims[0:chunk] * decay, 0.0)
            t_inv = _unit_lower_inverse(a_mat, INV_PASSES)
            eg = jnp.exp(gc)
            rhs = jnp.concatenate([v * beta, k * (beta * eg)], axis=1).astype(BF16)
            uw = _dot(t_inv.astype(BF16), rhs)
            u = uw[:, 0:hd]
            w = uw[:, hd:2 * hd]
            st = state[hh]
            ws_qs = _dot(jnp.concatenate([w, q * eg], axis=0).astype(BF16), st.astype(BF16))
            v_new = u - ws_qs[0:chunk]
            v_new_b = v_new.astype(BF16)
            qk = jnp.where(tri, sims[chunk:2 * chunk] * decay, 0.0)
            o = ws_qs[chunk:2 * chunk] + _dot(qk.astype(BF16), v_new_b)
            kd = (k * jnp.exp(g_last - gc)).astype(BF16)
            state[hh] = st * jnp.exp(g_last) + _dot_tn(kd, v_new_b)
            zz = z_ref[rows, cols].astype(F32)
            o = o * lax.rsqrt(jnp.mean(o * o, axis=-1, keepdims=True) + EPS) * nw * _silu(zz)
            o_ref[rows, cols] = o.astype(o_ref.dtype)
        return carry

    lax.fori_loop(0, ts // chunk, chunk_step, 0)


def delta_net(proj, gt, gr, conv_w, norm_w, *, seq, q_off, k_off, v_off, z_off, n_heads):
    b = proj.shape[0]
    hd = HEAD_DIM
    hg = 2
    ts = _tile(seq, 512)
    gw = hg * hd
    kw = n_heads * hd
    assert n_heads % hg == 0
    for off in (q_off, k_off, v_off, z_off, kw):
        assert off % gw == 0
    kern = functools.partial(_dn_kernel, ts=ts, hg=hg, chunk=DN_CHUNK)

    def col_spec(off):
        return pl.BlockSpec((None, ts, gw), lambda bi, g, s: (bi, s, off // gw + g))

    def w_spec(off):
        return pl.BlockSpec((DN_CONV, gw), lambda bi, g, s: (0, off // gw + g))

    block_bytes = (5 * ts * gw * 2 + ts * V7X_LANES * 4 + (ts // DN_CHUNK) * 16 * V7X_LANES * 4
                   + 3 * (ts + V7X_SUBLANES) * gw * 4 + 3 * ts * gw * 4)
    return pl.pallas_call(
        kern,
        grid=(b, n_heads // hg, seq // ts),
        in_specs=[col_spec(q_off), col_spec(k_off), col_spec(v_off), col_spec(z_off),
                  w_spec(0), w_spec(kw), w_spec(2 * kw),
                  pl.BlockSpec((None, ts, V7X_LANES), lambda bi, g, s: (bi, s, 0)),
                  pl.BlockSpec((None, ts // DN_CHUNK, n_heads, DN_CHUNK), lambda bi, g, s: (bi, s, 0, 0)),
                  pl.BlockSpec((1, hd), lambda bi, g, s: (0, 0))],
        out_specs=pl.BlockSpec((None, ts, gw), lambda bi, g, s: (bi, s, g)),
        out_shape=jax.ShapeDtypeStruct((b, seq, kw), BF16),
        scratch_shapes=[pltpu.VMEM((ts + V7X_SUBLANES, gw), F32)] * 3
                       + [pltpu.VMEM((ts, gw), F32)] * 3
                       + [pltpu.VMEM((hg, hd, hd), F32)],
        compiler_params=_params(("parallel", "parallel", "arbitrary"), block_bytes),
        name="delta_net",
    )(proj, proj, proj, proj, conv_w, conv_w, conv_w, gt, gr, norm_w.reshape(1, hd))


def _merge_kernel(oa_ref, od_ref, ga_ref, gd_ref, wa_ref, wd_ref, y_ref):
    a = _dot(oa_ref[...], wa_ref[...])
    d = _dot(od_ref[...], wd_ref[...])
    y = jax.nn.sigmoid(ga_ref[...].astype(F32)) * a + jax.nn.sigmoid(gd_ref[...].astype(F32)) * d
    y_ref[...] = y.astype(y_ref.dtype)


def branch_merge(o_a, o_d, proj, w_a, w_d, *, ga_off, gd_off):
    t, ka = o_a.shape
    kd = o_d.shape[1]
    n = w_a.shape[1]
    tm = _tile(t, 1024)
    tn = _tile(n, 512)
    assert ga_off % tn == 0 and gd_off % tn == 0
    block_bytes = 2 * (tm * ka + tm * kd + 3 * tm * tn + ka * tn + kd * tn)
    return pl.pallas_call(
        _merge_kernel,
        grid=(t // tm, n // tn),
        in_specs=[pl.BlockSpec((tm, ka), lambda i, j: (i, 0)),
                  pl.BlockSpec((tm, kd), lambda i, j: (i, 0)),
                  pl.BlockSpec((tm, tn), lambda i, j: (i, ga_off // tn + j)),
                  pl.BlockSpec((tm, tn), lambda i, j: (i, gd_off // tn + j)),
                  pl.BlockSpec((ka, tn), lambda i, j: (0, j)),
                  pl.BlockSpec((kd, tn), lambda i, j: (0, j))],
        out_specs=pl.BlockSpec((tm, tn), lambda i, j: (i, j)),
        out_shape=jax.ShapeDtypeStruct((t, n), BF16),
        compiler_params=_params(("parallel", "parallel"), block_bytes),
        name="branch_merge",
    )(o_a, o_d, proj, proj, w_a, w_d)


def _proj_norm_res_kernel(*refs, nk, with_next):
    if with_next:
        x_ref, w_ref, h_ref, post_ref, next_ref, h_out, xn_out = refs[:7]
        rest = refs[7:]
    else:
        x_ref, w_ref, h_ref, post_ref, h_out = refs[:5]
        next_ref = xn_out = None
        rest = refs[5:]

    def epilogue(f):
        hn = h_ref[...] + _rms(f, post_ref[...])
        h_out[...] = hn
        if with_next:
            xn_out[...] = _rms(hn, next_ref[...]).astype(xn_out.dtype)

    if nk == 1:
        epilogue(_dot(x_ref[...], w_ref[...]))
        return
    acc = rest[0]
    kk = pl.program_id(1)

    @pl.when(kk == 0)
    def _():
        acc[...] = _dot(x_ref[...], w_ref[...])

    @pl.when((kk > 0) & (kk < nk - 1))
    def _():
        acc[...] += _dot(x_ref[...], w_ref[...])

    @pl.when(kk == nk - 1)
    def _():
        epilogue(acc[...] + _dot(x_ref[...], w_ref[...]))


def proj_norm_residual(x, w, h, post_w, next_w, *, tk):
    t, k = x.shape
    d = w.shape[1]
    tm = _tile(t, 512)
    tk = _tile(k, tk)
    nk = k // tk
    with_next = next_w is not None
    kern = functools.partial(_proj_norm_res_kernel, nk=nk, with_next=with_next)
    row = pl.BlockSpec((1, d), lambda i, kk: (0, 0))
    tile = pl.BlockSpec((tm, d), lambda i, kk: (i, 0))
    in_specs = [pl.BlockSpec((tm, tk), lambda i, kk: (i, kk)),
                pl.BlockSpec((tk, d), lambda i, kk: (kk, 0)), tile, row]
    args = [x, w, h, post_w.reshape(1, d)]
    out_specs = [tile]
    out_shape = [jax.ShapeDtypeStruct((t, d), F32)]
    if with_next:
        in_specs.append(row)
        args.append(next_w.reshape(1, d))
        out_specs.append(tile)
        out_shape.append(jax.ShapeDtypeStruct((t, d), BF16))
    block_bytes = tm * tk * 2 + tk * d * 2 + tm * d * (4 + 4 + 2) + (tm * d * 2 if nk > 1 else 0)
    res = pl.pallas_call(
        kern,
        grid=(t // tm, nk),
        in_specs=in_specs,
        out_specs=out_specs,
        out_shape=out_shape,
        scratch_shapes=[pltpu.VMEM((tm, d), F32)] if nk > 1 else [],
        compiler_params=_params(("parallel", "arbitrary"), block_bytes),
        name="proj_norm_residual",
    )(*args)
    return (res[0], res[1]) if with_next else (res[0], None)


def _ffn_up_kernel(x_ref, xh_ref, wg_ref, wv_ref, cg_ref, cv_ref, bg_ref, bv_ref, o_ref, gbuf, vbuf,
                   *, tm, tiles_per_seq):
    i = pl.program_id(0)
    halo = V7X_SUBLANES
    keep = ((i % tiles_per_seq) != 0).astype(F32)
    x = x_ref[...]
    xh = xh_ref[...]

    def conv(w_ref, c_ref, b_ref, buf):
        w = w_ref[...]
        buf[0:halo, :] = _dot(xh, w) * keep
        buf[halo:halo + tm, :] = _dot(x, w)
        cw = c_ref[...]
        y = buf[halo:halo + tm, :] * cw[FFN_CONV - 1:FFN_CONV, :] + b_ref[...]
        for j in range(FFN_CONV - 1):
            lag = FFN_CONV - 1 - j
            y = y + buf[halo - lag:halo - lag + tm, :] * cw[j:j + 1, :]
        return y

    gate = conv(wg_ref, cg_ref, bg_ref, gbuf)
    val = conv(wv_ref, cv_ref, bv_ref, vbuf)
    o_ref[...] = (_silu(gate) * val).astype(o_ref.dtype)


def ffn_up(xn, w_up, conv_w, conv_b, *, seq):
    t, d = xn.shape
    dff = w_up.shape[1] // 2
    tm = _tile(seq, 1024)
    tn = _tile(dff, 512)
    halo = V7X_SUBLANES
    nj = dff // tn
    kern = functools.partial(_ffn_up_kernel, tm=tm, tiles_per_seq=seq // tm)
    block_bytes = 2 * (tm * d + halo * d + 2 * d * tn + tm * tn) + 2 * (tm + halo) * tn * 4
    return pl.pallas_call(
        kern,
        grid=(t // tm, nj),
        in_specs=[pl.BlockSpec((tm, d), lambda i, j: (i, 0)),
                  pl.BlockSpec((halo, d), lambda i, j: (jnp.maximum(i * (tm // halo) - 1, 0), 0)),
                  pl.BlockSpec((d, tn), lambda i, j: (0, j)),
                  pl.BlockSpec((d, tn), lambda i, j: (0, nj + j)),
                  pl.BlockSpec((FFN_CONV, tn), lambda i, j: (0, j)),
                  pl.BlockSpec((FFN_CONV, tn), lambda i, j: (0, nj + j)),
                  pl.BlockSpec((1, tn), lambda i, j: (0, j)),
                  pl.BlockSpec((1, tn), lambda i, j: (0, nj + j))],
        out_specs=pl.BlockSpec((tm, tn), lambda i, j: (i, j)),
        out_shape=jax.ShapeDtypeStruct((t, dff), BF16),
        scratch_shapes=[pltpu.VMEM((tm + halo, tn), F32)] * 2,
        compiler_params=_params(("parallel", "parallel"), block_bytes),
        name="ffn_up",
    )(xn, xn, w_up, w_up, conv_w, conv_w, conv_b.reshape(1, 2 * dff), conv_b.reshape(1, 2 * dff))


def kernel(x, positions, norm_mix_pre, w_in, attn_sinks, dn_conv_w, dn_a_log, dn_dt_bias, dn_norm_w,
           w_branch_attn, w_branch_dn, w_out, norm_mix_post, norm_ffn_pre, w_ffn_up, ffn_conv_w,
           ffn_conv_b, w_ffn_down, norm_ffn_post):
    b, seq, d = x.shape
    depth = w_in.shape[0]
    t = b * seq
    n_attn = attn_sinks.shape[1]
    n_dn = dn_a_log.shape[1]
    qa_w = n_attn * HEAD_DIM
    kv_w = ATTN_KV_HEADS * HEAD_DIM
    dn_w = n_dn * HEAD_DIM

    edges = [0]
    for wdt in (qa_w, kv_w, kv_w, 3 * dn_w, n_dn, n_dn, dn_w, d, d):
        edges.append(edges[-1] + wdt)
    main_cols = [(edges[0], edges[4]), (edges[6], edges[9])]
    w_main = jnp.concatenate([w_in[:, :, lo:hi] for lo, hi in main_cols], axis=-1).astype(BF16)
    w_gate = jnp.concatenate(
        [w_in[:, :, edges[4]:edges[6]],
         jnp.zeros((depth, d, V7X_LANES - 2 * n_dn), w_in.dtype)], axis=-1).astype(BF16)
    q_off, k_off, v_off = 0, qa_w, qa_w + kv_w
    dq_off = qa_w + 2 * kv_w
    dk_off, dv_off, z_off = dq_off + dn_w, dq_off + 2 * dn_w, dq_off + 3 * dn_w
    ga_off, gd_off = z_off + dn_w, z_off + dn_w + d
    main_w = gd_off + d

    w_ba = w_branch_attn.astype(BF16)
    w_bd = w_branch_dn.astype(BF16)
    w_o = w_out.astype(BF16)
    w_up = w_ffn_up.astype(BF16)
    w_dn = w_ffn_down.astype(BF16)

    cos, sin = rope_tables(positions)
    cos = cos.reshape(b, seq, HEAD_DIM)
    sin = sin.reshape(b, seq, HEAD_DIM)

    h = x.reshape(t, d)
    xn = rmsnorm(h, norm_mix_pre[0])
    for l in range(depth):
        proj = matmul(xn, w_main[l])
        proj3 = proj.reshape(b, seq, main_w)
        o_a = attention(proj3, cos, sin, attn_sinks[l], seq=seq, q_off=q_off, k_off=k_off,
                        v_off=v_off, n_heads=n_attn)
        gt, gr = dn_gates(xn, w_gate[l], dn_a_log[l], dn_dt_bias[l], n_heads=n_dn)
        o_d = delta_net(proj3, gt.reshape(b, seq, V7X_LANES),
                        gr.reshape(b, seq // DN_CHUNK, n_dn, DN_CHUNK), dn_conv_w[l], dn_norm_w[l],
                        seq=seq, q_off=dq_off, k_off=dk_off, v_off=dv_off, z_off=z_off, n_heads=n_dn)
        y = branch_merge(o_a.reshape(t, qa_w), o_d.reshape(t, dn_w), proj, w_ba[l], w_bd[l],
                         ga_off=ga_off, gd_off=gd_off)
        h, xf = proj_norm_residual(y, w_o[l], h, norm_mix_post[l], norm_ffn_pre[l], tk=d)
        u = ffn_up(xf, w_up[l], ffn_conv_w[l], ffn_conv_b[l], seq=seq)
        next_w = norm_mix_pre[l + 1] if l + 1 < depth else None
        h, xn = proj_norm_residual(u, w_dn[l], h, norm_ffn_post[l], next_w, tk=1408)
    return h.reshape(b, seq, d)
```

```python
import functools

import jax
import jax.numpy as jnp
from jax import lax
from jax.experimental import pallas as pl
from jax.experimental.pallas import tpu as pltpu

HEAD_DIM = 128
ATTN_KV_HEADS = 4
ATTN_BLOCK = 128
ROT_DIM = HEAD_DIM // 4
ROPE_THETA = 500000.0
DN_CHUNK = 64
DN_CONV = 4
FFN_CONV = 3
EPS = 1e-6

V7X_LANES = 128
V7X_SUBLANES = 8
V7X_VMEM_BYTES = 64 * 1024 * 1024

DN_PACK = 2 * V7X_LANES // DN_CHUNK
DN_HEADS_PER_STEP = 8
INV_PASSES = 1
INV_BASE = 16

F32 = jnp.float32
BF16 = jnp.bfloat16


def _dot(a, b):
    return jnp.dot(a, b, preferred_element_type=F32)


def _dot_nt(a, b):
    return lax.dot_general(a, b, (((1,), (1,)), ((), ())), preferred_element_type=F32)


def _dot_tn(a, b):
    return lax.dot_general(a, b, (((0,), (0,)), ((), ())), preferred_element_type=F32)


def _rms(x, w):
    return x * lax.rsqrt(jnp.mean(x * x, axis=-1, keepdims=True) + EPS) * w


def _silu(x):
    return x * jax.nn.sigmoid(x)


def _vmem_limit(block_bytes):
    return int(min(V7X_VMEM_BYTES - (2 << 20), 2 * block_bytes + (12 << 20)))


def _params(semantics, block_bytes):
    return pltpu.CompilerParams(dimension_semantics=semantics,
                                vmem_limit_bytes=_vmem_limit(block_bytes))


def _tile(n, want, align=V7X_LANES):
    for t in range(min(n, want) // align * align, 0, -align):
        if n % t == 0:
            return t
    raise ValueError(f"no {align}-aligned tile divides {n}")


def _rmsnorm_kernel(x_ref, w_ref, o_ref):
    o_ref[...] = _rms(x_ref[...], w_ref[...]).astype(o_ref.dtype)


def rmsnorm(x, w):
    t, d = x.shape
    tm = _tile(t, 512)
    return pl.pallas_call(
        _rmsnorm_kernel,
        grid=(t // tm,),
        in_specs=[pl.BlockSpec((tm, d), lambda i: (i, 0)),
                  pl.BlockSpec((1, d), lambda i: (0, 0))],
        out_specs=pl.BlockSpec((tm, d), lambda i: (i, 0)),
        out_shape=jax.ShapeDtypeStruct((t, d), BF16),
        compiler_params=_params(("parallel",), tm * d * 6),
        name="rmsnorm",
    )(x, w.reshape(1, d))


def _matmul_kernel(x_ref, w_ref, o_ref):
    o_ref[...] = _dot(x_ref[...], w_ref[...]).astype(o_ref.dtype)


def matmul(x, w):
    t, k = x.shape
    n = w.shape[1]
    tm = _tile(t, 1024)
    tn = _tile(n, 1024)
    return pl.pallas_call(
        _matmul_kernel,
        grid=(t // tm, n // tn),
        in_specs=[pl.BlockSpec((tm, k), lambda i, j: (i, 0)),
                  pl.BlockSpec((k, tn), lambda i, j: (0, j))],
        out_specs=pl.BlockSpec((tm, tn), lambda i, j: (i, j)),
        out_shape=jax.ShapeDtypeStruct((t, n), BF16),
        compiler_params=_params(("parallel", "parallel"), 2 * (tm * k + k * tn + tm * tn)),
        name="in_proj",
    )(x, w)


def _rope_table_kernel(pos_ref, freq_ref, sign_ref, cos_ref, sin_ref):
    ang = pos_ref[...] * freq_ref[...]
    cos_ref[...] = jnp.cos(ang)
    sin_ref[...] = jnp.sin(ang) * sign_ref[...]


def rope_tables(positions):
    t = positions.size
    half = ROT_DIM // 2
    inv_freq = ROPE_THETA ** (-jnp.arange(0, ROT_DIM, 2, dtype=F32) / ROT_DIM)
    freq = jnp.concatenate([inv_freq, inv_freq, jnp.zeros((HEAD_DIM - ROT_DIM,), F32)]).reshape(1, HEAD_DIM)
    sign = jnp.concatenate([-jnp.ones((half,), F32), jnp.ones((HEAD_DIM - half,), F32)]).reshape(1, HEAD_DIM)
    pos = positions.astype(F32).reshape(t, 1)
    tm = _tile(t, 1024)
    row = pl.BlockSpec((1, HEAD_DIM), lambda i: (0, 0))
    out = pl.BlockSpec((tm, HEAD_DIM), lambda i: (i, 0))
    return pl.pallas_call(
        _rope_table_kernel,
        grid=(t // tm,),
        in_specs=[pl.BlockSpec((tm, 1), lambda i: (i, 0)), row, row],
        out_specs=[out, out],
        out_shape=[jax.ShapeDtypeStruct((t, HEAD_DIM), F32)] * 2,
        compiler_params=_params(("parallel",), 3 * tm * HEAD_DIM * 4),
        name="rope_tables",
    )(pos, freq, sign)


def _rotate_half_matrix():
    half = ROT_DIM // 2
    r = jnp.arange(HEAD_DIM)[:, None]
    c = jnp.arange(HEAD_DIM)[None, :]
    p = ((c < half) & (r == c + half)) | ((c >= half) & (c < ROT_DIM) & (r == c - half))
    return p.astype(BF16)


def _attn_kernel(sink_ref, q_ref, k_ref, v_ref, cos_ref, sin_ref, perm_ref, o_ref, kbuf, vbuf,
                 *, tq, n_kv, group):
    i = pl.program_id(1)
    blk = ATTN_BLOCK
    hd = HEAD_DIM

    @pl.when(i == 0)
    def _():
        kbuf[0:blk, :] = jnp.zeros((blk, n_kv * hd), kbuf.dtype)
        vbuf[0:blk, :] = jnp.zeros((blk, n_kv * hd), vbuf.dtype)

    @pl.when(i > 0)
    def _():
        kbuf[0:blk, :] = kbuf[tq:tq + blk, :]
        vbuf[0:blk, :] = vbuf[tq:tq + blk, :]

    perm = perm_ref[...]

    def rope(xb, c, s):
        return (xb.astype(F32) * c + _dot(xb, perm) * s).astype(BF16)

    cos = cos_ref[...]
    sin = sin_ref[...]
    for kh in range(n_kv):
        cols = slice(kh * hd, (kh + 1) * hd)
        kbuf[blk:blk + tq, cols] = rope(k_ref[:, cols], cos, sin)
    vbuf[blk:blk + tq, :] = v_ref[...]

    r = lax.broadcasted_iota(jnp.int32, (group * blk, 2 * blk), 0) % blk
    c = lax.broadcasted_iota(jnp.int32, (group * blk, 2 * blk), 1)
    band = (c > r) & (c <= r + blk)
    band_first = band & (c >= blk)
    scale = hd ** -0.5

    for j in range(tq // blk):
        rows = slice(j * blk, (j + 1) * blk)
        cj = cos[rows]
        sj = sin[rows]
        mask = band if j > 0 else (band_first | (band & (i > 0)))
        for kh in range(n_kv):
            cols = slice(kh * hd, (kh + 1) * hd)
            heads = [kh * group + g for g in range(group)]
            qs = jnp.concatenate(
                [rope(q_ref[rows, h * hd:(h + 1) * hd], cj, sj) for h in heads], axis=0)
            kk = kbuf[j * blk:(j + 2) * blk, cols]
            vv = vbuf[j * blk:(j + 2) * blk, cols]
            logits = jnp.where(mask, _dot_nt(qs, kk) * scale, -jnp.inf)
            sink = jnp.concatenate([jnp.full((blk, 1), sink_ref[h], F32) for h in heads], axis=0)
            m = jnp.maximum(jnp.max(logits, axis=-1, keepdims=True), sink)
            p = jnp.exp(logits - m)
            denom = jnp.sum(p, axis=-1, keepdims=True) + jnp.exp(sink - m)
            o = _dot(p.astype(BF16), vv) / denom
            for g, h in enumerate(heads):
                o_ref[rows, h * hd:(h + 1) * hd] = o[g * blk:(g + 1) * blk].astype(o_ref.dtype)


def attention(proj, cos, sin, sinks, *, seq, q_off, k_off, v_off, n_heads):
    b = proj.shape[0]
    n_kv = ATTN_KV_HEADS
    group = n_heads // n_kv
    qw = n_heads * HEAD_DIM
    kw = n_kv * HEAD_DIM
    tq = _tile(seq, 512)
    assert q_off % qw == 0 and k_off % kw == 0 and v_off % kw == 0
    kern = functools.partial(_attn_kernel, tq=tq, n_kv=n_kv, group=group)
    block_bytes = tq * (2 * qw + 2 * kw) * 2 + 2 * tq * HEAD_DIM * 4 + 2 * (tq + ATTN_BLOCK) * kw * 2
    return pl.pallas_call(
        kern,
        grid=(b, seq // tq),
        in_specs=[pl.BlockSpec(memory_space=pltpu.SMEM),
                  pl.BlockSpec((None, tq, qw), lambda bi, i: (bi, i, q_off // qw)),
                  pl.BlockSpec((None, tq, kw), lambda bi, i: (bi, i, k_off // kw)),
                  pl.BlockSpec((None, tq, kw), lambda bi, i: (bi, i, v_off // kw)),
                  pl.BlockSpec((None, tq, HEAD_DIM), lambda bi, i: (bi, i, 0)),
                  pl.BlockSpec((None, tq, HEAD_DIM), lambda bi, i: (bi, i, 0)),
                  pl.BlockSpec((HEAD_DIM, HEAD_DIM), lambda bi, i: (0, 0))],
        out_specs=pl.BlockSpec((None, tq, qw), lambda bi, i: (bi, i, 0)),
        out_shape=jax.ShapeDtypeStruct((b, seq, qw), BF16),
        scratch_shapes=[pltpu.VMEM((tq + ATTN_BLOCK, kw), BF16),
                        pltpu.VMEM((tq + ATTN_BLOCK, kw), BF16)],
        compiler_params=_params(("parallel", "arbitrary"), block_bytes),
        name="swa_attention",
    )(sinks, proj, proj, proj, cos, sin, _rotate_half_matrix())


GATE_GROUPS = 5


def _gates_kernel(x_ref, w_ref, alog_ref, dt_ref, gt_ref, gr_ref, *, n_heads, chunk):
    tm = x_ref.shape[0]
    pw = DN_PACK * chunk
    val = _dot(x_ref[...], w_ref[...])
    lane = lax.broadcasted_iota(jnp.int32, val.shape, 1)
    beta = jax.nn.sigmoid(val)
    z = val + dt_ref[...]
    softplus = jnp.maximum(z, 0.0) + jnp.log1p(jnp.exp(-jnp.abs(z)))
    g = jnp.where((lane >= n_heads) & (lane < GATE_GROUPS * n_heads), -jnp.exp(alog_ref[...]) * softplus, 0.0)
    r = lax.broadcasted_iota(jnp.int32, (tm, tm), 0)
    c = lax.broadcasted_iota(jnp.int32, (tm, tm), 1)
    same = (r // chunk) == (c // chunk)
    tri = (same & (c <= r)).astype(BF16)
    ones = same.astype(BF16)
    hi = g.astype(BF16)
    rem = g - hi.astype(F32)
    mid = rem.astype(BF16)
    lo = (rem - mid.astype(F32)).astype(BF16)
    gc = _dot(tri, hi) + _dot(tri, mid) + _dot(tri, lo)
    total = _dot(ones, hi) + _dot(ones, mid) + _dot(ones, lo)
    group = lane // n_heads
    table = jnp.where(group == 0, beta,
            jnp.where(group == 1, gc,
            jnp.where(group == 2, jnp.exp(gc),
            jnp.where(group == 3, jnp.exp(total - gc),
            jnp.where(group == 4, jnp.exp(total), 0.0)))))
    gt_ref[...] = table
    gct = gc.T
    for pi in range(tm // pw):
        gr_ref[pi] = gct[n_heads:2 * n_heads, pi * pw:(pi + 1) * pw]


def dn_gates(xn, w_gate, a_log, dt_bias, *, n_heads):
    t, d = xn.shape
    pw = DN_PACK * DN_CHUNK
    tm = _tile(t, 512)
    assert tm % pw == 0 and GATE_GROUPS * n_heads <= V7X_LANES
    pad = jnp.zeros((V7X_LANES - GATE_GROUPS * n_heads,), F32)
    zero = jnp.zeros((n_heads,), F32)
    alog = jnp.concatenate([zero] + [a_log] * (GATE_GROUPS - 1) + [pad]).reshape(1, V7X_LANES)
    dt = jnp.concatenate([zero] + [dt_bias] * (GATE_GROUPS - 1) + [pad]).reshape(1, V7X_LANES)
    row = pl.BlockSpec((1, V7X_LANES), lambda i: (0, 0))
    kern = functools.partial(_gates_kernel, n_heads=n_heads, chunk=DN_CHUNK)
    return pl.pallas_call(
        kern,
        grid=(t // tm,),
        in_specs=[pl.BlockSpec((tm, d), lambda i: (i, 0)),
                  pl.BlockSpec((d, V7X_LANES), lambda i: (0, 0)), row, row],
        out_specs=[pl.BlockSpec((tm, V7X_LANES), lambda i: (i, 0)),
                   pl.BlockSpec((tm // pw, n_heads, pw), lambda i: (i, 0, 0))],
        out_shape=[jax.ShapeDtypeStruct((t, V7X_LANES), F32),
                   jax.ShapeDtypeStruct((t // pw, n_heads, pw), F32)],
        compiler_params=_params(("parallel",), tm * d * 2 + d * V7X_LANES * 2 + 2 * tm * V7X_LANES * 4
                                + 4 * tm * tm),
        name="dn_gates",
    )(xn, w_gate, alog, dt)


def _block_diag(xp):
    n, m = xp.shape
    tiled = jnp.concatenate([xp] * (m // n), axis=0)
    r = lax.broadcasted_iota(jnp.int32, (m, m), 0) // n
    c = lax.broadcasted_iota(jnp.int32, (m, m), 1) // n
    return jnp.where(r == c, tiled, jnp.zeros_like(tiled))


def _packed_dot(xp, yp, passes):
    xh = xp.astype(BF16)
    yh = yp.astype(BF16)
    out = _dot(xh, _block_diag(yh))
    if passes >= 3:
        xl = (xp - xh.astype(F32)).astype(BF16)
        yl = (yp - yh.astype(F32)).astype(BF16)
        out = out + _dot(xh, _block_diag(yl)) + _dot(xl, _block_diag(yh))
    return out


def _packed_unit_lower_inverses(aps, passes):
    n = aps[0].shape[0]
    r = lax.broadcasted_iota(jnp.int32, aps[0].shape, 0)
    c = lax.broadcasted_iota(jnp.int32, aps[0].shape, 1) % n
    eye = (r == c).astype(F32)
    size = INV_BASE
    in_blk = (r // size) == (c // size)
    xs = [jnp.where(in_blk, -ap, 0.0) for ap in aps]
    invs = [eye + x for x in xs]
    power = 1
    while 2 * power < size:
        xs = [_packed_dot(x, x, passes) for x in xs]
        invs = [inv + _packed_dot(inv, x, passes) for inv, x in zip(invs, xs)]
        power *= 2
    while size < n:
        merged = (r // (2 * size)) == (c // (2 * size))
        offs = [_packed_dot(jnp.where(merged & ~in_blk, ap, 0.0), inv, passes) for ap, inv in zip(aps, invs)]
        invs = [inv - _packed_dot(inv, off, passes) for inv, off in zip(invs, offs)]
        in_blk = merged
        size *= 2
    return invs


def _dn_kernel(q_ref, k_ref, v_ref, z_ref, wq_ref, wk_ref, wv_ref, gt_ref, gr_ref, nw_ref, o_ref,
               qbuf, kbuf, vbuf, qn, kn, vn, rhs_s, u_s, wq_s, kd_s, qkm_s, vnew_s, qs_s, state,
               *, ts, hg, chunk):
    s = pl.program_id(2)
    gi = pl.program_id(1)
    hd = HEAD_DIM
    halo = V7X_SUBLANES

    @pl.when(s == 0)
    def _():
        state[...] = jnp.zeros(state.shape, F32)
        for buf in (qbuf, kbuf, vbuf):
            buf[0:halo, :] = jnp.zeros((halo, hg * hd), F32)

    @pl.when(s > 0)
    def _():
        for buf in (qbuf, kbuf, vbuf):
            buf[0:halo, :] = buf[ts:ts + halo, :]

    def conv_silu(src_ref, buf, w_ref):
        buf[halo:halo + ts, :] = src_ref[...].astype(F32)
        w = w_ref[...]
        y = buf[halo:halo + ts, :] * w[DN_CONV - 1:DN_CONV, :]
        for j in range(DN_CONV - 1):
            lag = DN_CONV - 1 - j
            y = y + buf[halo - lag:halo - lag + ts, :] * w[j:j + 1, :]
        return _silu(y)

    def l2norm(x):
        return x * lax.rsqrt(jnp.sum(x * x, axis=-1, keepdims=True) + EPS)

    qc = conv_silu(q_ref, qbuf, wq_ref)
    kc = conv_silu(k_ref, kbuf, wk_ref)
    vn[...] = conv_silu(v_ref, vbuf, wv_ref)
    for hh in range(hg):
        cols = slice(hh * hd, (hh + 1) * hd)
        qn[:, cols] = l2norm(qc[:, cols]) * (hd ** -0.5)
        kn[:, cols] = l2norm(kc[:, cols])

    pack = DN_PACK
    pw = pack * chunk
    n_pack = ts // pw
    n_heads = gr_ref.shape[1]
    r = lax.broadcasted_iota(jnp.int32, (pw, pw), 0)
    c = lax.broadcasted_iota(jnp.int32, (pw, pw), 1)
    same = (r // chunk) == (c // chunk)
    tri = same & (r >= c)
    strict = same & (r > c)
    lane = lax.broadcasted_iota(jnp.int32, (pw, V7X_LANES), 1)
    nw = nw_ref[...]

    jobs = [(hh, p) for hh in range(hg) for p in range(n_pack)]
    chunk_decay = {}
    aps = []
    for hh, p in jobs:
        cols = slice(hh * hd, (hh + 1) * hd)
        rows = slice(p * pw, (p + 1) * pw)
        h = gi * hg + hh
        gt = gt_ref[rows, :]

        def gate_col(group, gt=gt, h=h):
            return jnp.sum(jnp.where(lane == group * n_heads + h, gt, 0.0), axis=-1, keepdims=True)

        beta, gc, eg, erem, etot = (gate_col(i) for i in range(GATE_GROUPS))
        gc_row = gr_ref[p, pl.ds(h, 1), :]
        q = qn[rows, cols]
        k = kn[rows, cols]
        v = vn[rows, cols]
        kb = k.astype(BF16)
        sims = _dot_nt(jnp.concatenate([kb, q.astype(BF16)], axis=0), kb)
        decay = jnp.exp(jnp.where(tri, gc - gc_row, -jnp.inf))
        a_full = jnp.where(strict, beta * sims[0:pw] * decay, 0.0)
        ap = a_full[0:chunk]
        for bb in range(1, pack):
            ap = ap + a_full[bb * chunk:(bb + 1) * chunk]
        aps.append(ap)
        rhs_s[hh, rows, :] = jnp.concatenate([v * beta, k * (beta * eg)], axis=1).astype(BF16)
        qg = (q * eg).astype(BF16)
        for bb in range(pack):
            wq_s[hh, p * pack + bb, chunk:2 * chunk, :] = qg[bb * chunk:(bb + 1) * chunk]
            chunk_decay[(hh, p * pack + bb)] = etot[bb * chunk:bb * chunk + 1, :]
        kd_s[hh, rows, :] = (k * erem).astype(BF16)
        qkm_s[hh, p] = (sims[pw:2 * pw] * decay).astype(BF16)

    tps = _packed_unit_lower_inverses(aps, INV_PASSES)

    for (hh, p), tp in zip(jobs, tps):
        rows = slice(p * pw, (p + 1) * pw)
        uw = _dot(_block_diag(tp.astype(BF16)), rhs_s[hh, rows, :])
        u_s[hh, rows, :] = uw[:, 0:hd]
        w = uw[:, hd:2 * hd].astype(BF16)
        for bb in range(pack):
            wq_s[hh, p * pack + bb, 0:chunk, :] = w[bb * chunk:(bb + 1) * chunk]

    for cidx in range(ts // chunk):
        rows = slice(cidx * chunk, (cidx + 1) * chunk)
        for hh in range(hg):
            st = state[hh]
            ws_qs = _dot(wq_s[hh, cidx], st.astype(BF16))
            v_new = (u_s[hh, rows, :] - ws_qs[0:chunk]).astype(BF16)
            vnew_s[hh, rows, :] = v_new
            qs_s[hh, rows, :] = ws_qs[chunk:2 * chunk]
            state[hh] = st * chunk_decay[(hh, cidx)] + _dot_tn(kd_s[hh, rows, :], v_new)

    for hh in range(hg):
        cols = slice(hh * hd, (hh + 1) * hd)
        for p in range(n_pack):
            rows = slice(p * pw, (p + 1) * pw)
            o = qs_s[hh, rows, :] + _dot(qkm_s[hh, p], vnew_s[hh, rows, :])
            zz = z_ref[rows, cols].astype(F32)
            o = o * lax.rsqrt(jnp.mean(o * o, axis=-1, keepdims=True) + EPS) * nw * _silu(zz)
            o_ref[rows, cols] = o.astype(o_ref.dtype)


def delta_net(proj, gt, gr, conv_w, norm_w, *, seq, q_off, k_off, v_off, z_off, n_heads):
    b = proj.shape[0]
    hd = HEAD_DIM
    hg = min(DN_HEADS_PER_STEP, n_heads)
    pw = DN_PACK * DN_CHUNK
    ts = _tile(seq, 512, align=pw)
    gw = hg * hd
    kw = n_heads * hd
    assert n_heads % hg == 0
    for off in (q_off, k_off, v_off, z_off, kw):
        assert off % gw == 0
    kern = functools.partial(_dn_kernel, ts=ts, hg=hg, chunk=DN_CHUNK)

    def col_spec(off):
        return pl.BlockSpec((None, ts, gw), lambda bi, g, s: (bi, s, off // gw + g))

    def w_spec(off):
        return pl.BlockSpec((DN_CONV, gw), lambda bi, g, s: (0, off // gw + g))

    scratch = ([pltpu.VMEM((ts + V7X_SUBLANES, gw), F32)] * 3
               + [pltpu.VMEM((ts, gw), F32)] * 3
               + [pltpu.VMEM((hg, ts, 2 * hd), BF16),
                  pltpu.VMEM((hg, ts, hd), F32),
                  pltpu.VMEM((hg, ts // DN_CHUNK, 2 * DN_CHUNK, hd), BF16),
                  pltpu.VMEM((hg, ts, hd), BF16),
                  pltpu.VMEM((hg, ts // pw, pw, pw), BF16),
                  pltpu.VMEM((hg, ts, hd), BF16),
                  pltpu.VMEM((hg, ts, hd), F32),
                  pltpu.VMEM((hg, hd, hd), F32)])
    block_bytes = (5 * ts * gw * 2 + ts * V7X_LANES * 4 + (ts // pw) * n_heads * pw * 4
                   + 3 * (ts + V7X_SUBLANES) * gw * 4 + 3 * ts * gw * 4 + hg * ts * hd * 18)
    return pl.pallas_call(
        kern,
        grid=(b, n_heads // hg, seq // ts),
        in_specs=[col_spec(q_off), col_spec(k_off), col_spec(v_off), col_spec(z_off),
                  w_spec(0), w_spec(kw), w_spec(2 * kw),
                  pl.BlockSpec((None, ts, V7X_LANES), lambda bi, g, s: (bi, s, 0)),
                  pl.BlockSpec((None, ts // pw, n_heads, pw), lambda bi, g, s: (bi, s, 0, 0)),
                  pl.BlockSpec((1, hd), lambda bi, g, s: (0, 0))],
        out_specs=pl.BlockSpec((None, ts, gw), lambda bi, g, s: (bi, s, g)),
        out_shape=jax.ShapeDtypeStruct((b, seq, kw), BF16),
        scratch_shapes=scratch,
        compiler_params=_params(("parallel", "parallel", "arbitrary"), block_bytes),
        name="delta_net",
    )(proj, proj, proj, proj, conv_w, conv_w, conv_w, gt, gr, norm_w.reshape(1, hd))


def _merge_kernel(oa_ref, od_ref, ga_ref, gd_ref, wa_ref, wd_ref, y_ref):
    a = _dot(oa_ref[...], wa_ref[...])
    d = _dot(od_ref[...], wd_ref[...])
    y = jax.nn.sigmoid(ga_ref[...].astype(F32)) * a + jax.nn.sigmoid(gd_ref[...].astype(F32)) * d
    y_ref[...] = y.astype(y_ref.dtype)


def branch_merge(o_a, o_d, proj, w_a, w_d, *, ga_off, gd_off):
    t, ka = o_a.shape
    kd = o_d.shape[1]
    n = w_a.shape[1]
    tm = _tile(t, 1024)
    tn = _tile(n, 512)
    assert ga_off % tn == 0 and gd_off % tn == 0
    block_bytes = 2 * (tm * ka + tm * kd + 3 * tm * tn + ka * tn + kd * tn)
    return pl.pallas_call(
        _merge_kernel,
        grid=(t // tm, n // tn),
        in_specs=[pl.BlockSpec((tm, ka), lambda i, j: (i, 0)),
                  pl.BlockSpec((tm, kd), lambda i, j: (i, 0)),
                  pl.BlockSpec((tm, tn), lambda i, j: (i, ga_off // tn + j)),
                  pl.BlockSpec((tm, tn), lambda i, j: (i, gd_off // tn + j)),
                  pl.BlockSpec((ka, tn), lambda i, j: (0, j)),
                  pl.BlockSpec((kd, tn), lambda i, j: (0, j))],
        out_specs=pl.BlockSpec((tm, tn), lambda i, j: (i, j)),
        out_shape=jax.ShapeDtypeStruct((t, n), BF16),
        compiler_params=_params(("parallel", "parallel"), block_bytes),
        name="branch_merge",
    )(o_a, o_d, proj, proj, w_a, w_d)


def _proj_norm_res_kernel(*refs, nk, with_next):
    if with_next:
        x_ref, w_ref, h_ref, post_ref, next_ref, h_out, xn_out = refs[:7]
        rest = refs[7:]
    else:
        x_ref, w_ref, h_ref, post_ref, h_out = refs[:5]
        next_ref = xn_out = None
        rest = refs[5:]

    def epilogue(f):
        hn = h_ref[...] + _rms(f, post_ref[...])
        h_out[...] = hn
        if with_next:
            xn_out[...] = _rms(hn, next_ref[...]).astype(xn_out.dtype)

    if nk == 1:
        epilogue(_dot(x_ref[...], w_ref[...]))
        return
    acc = rest[0]
    kk = pl.program_id(1)

    @pl.when(kk == 0)
    def _():
        acc[...] = _dot(x_ref[...], w_ref[...])

    @pl.when((kk > 0) & (kk < nk - 1))
    def _():
        acc[...] += _dot(x_ref[...], w_ref[...])

    @pl.when(kk == nk - 1)
    def _():
        epilogue(acc[...] + _dot(x_ref[...], w_ref[...]))


def proj_norm_residual(x, w, h, post_w, next_w, *, tk):
    t, k = x.shape
    d = w.shape[1]
    tm = _tile(t, 512)
    tk = _tile(k, tk)
    nk = k // tk
    with_next = next_w is not None
    kern = functools.partial(_proj_norm_res_kernel, nk=nk, with_next=with_next)
    row = pl.BlockSpec((1, d), lambda i, kk: (0, 0))
    tile = pl.BlockSpec((tm, d), lambda i, kk: (i, 0))
    in_specs = [pl.BlockSpec((tm, tk), lambda i, kk: (i, kk)),
                pl.BlockSpec((tk, d), lambda i, kk: (kk, 0)), tile, row]
    args = [x, w, h, post_w.reshape(1, d)]
    out_specs = [tile]
    out_shape = [jax.ShapeDtypeStruct((t, d), F32)]
    if with_next:
        in_specs.append(row)
        args.append(next_w.reshape(1, d))
        out_specs.append(tile)
        out_shape.append(jax.ShapeDtypeStruct((t, d), BF16))
    block_bytes = tm * tk * 2 + tk * d * 2 + tm * d * (4 + 4 + 2) + (tm * d * 2 if nk > 1 else 0)
    res = pl.pallas_call(
        kern,
        grid=(t // tm, nk),
        in_specs=in_specs,
        out_specs=out_specs,
        out_shape=out_shape,
        scratch_shapes=[pltpu.VMEM((tm, d), F32)] if nk > 1 else [],
        compiler_params=_params(("parallel", "arbitrary"), block_bytes),
        name="proj_norm_residual",
    )(*args)
    return (res[0], res[1]) if with_next else (res[0], None)


def _ffn_up_kernel(x_ref, xh_ref, wg_ref, wv_ref, cg_ref, cv_ref, bg_ref, bv_ref, o_ref, gbuf, vbuf,
                   *, tm, tiles_per_seq):
    i = pl.program_id(0)
    halo = V7X_SUBLANES
    keep = ((i % tiles_per_seq) != 0).astype(F32)
    x = x_ref[...]
    xh = xh_ref[...]

    def conv(w_ref, c_ref, b_ref, buf):
        w = w_ref[...]
        buf[0:halo, :] = _dot(xh, w) * keep
        buf[halo:halo + tm, :] = _dot(x, w)
        cw = c_ref[...]
        y = buf[halo:halo + tm, :] * cw[FFN_CONV - 1:FFN_CONV, :] + b_ref[...]
        for j in range(FFN_CONV - 1):
            lag = FFN_CONV - 1 - j
            y = y + buf[halo - lag:halo - lag + tm, :] * cw[j:j + 1, :]
        return y

    gate = conv(wg_ref, cg_ref, bg_ref, gbuf)
    val = conv(wv_ref, cv_ref, bv_ref, vbuf)
    o_ref[...] = (_silu(gate) * val).astype(o_ref.dtype)


def ffn_up(xn, w_up, conv_w, conv_b, *, seq):
    t, d = xn.shape
    dff = w_up.shape[1] // 2
    tm = _tile(seq, 1024)
    tn = _tile(dff, 512)
    halo = V7X_SUBLANES
    nj = dff // tn
    kern = functools.partial(_ffn_up_kernel, tm=tm, tiles_per_seq=seq // tm)
    block_bytes = 2 * (tm * d + halo * d + 2 * d * tn + tm * tn) + 2 * (tm + halo) * tn * 4
    return pl.pallas_call(
        kern,
        grid=(t // tm, nj),
        in_specs=[pl.BlockSpec((tm, d), lambda i, j: (i, 0)),
                  pl.BlockSpec((halo, d), lambda i, j: (jnp.maximum(i * (tm // halo) - 1, 0), 0)),
                  pl.BlockSpec((d, tn), lambda i, j: (0, j)),
                  pl.BlockSpec((d, tn), lambda i, j: (0, nj + j)),
                  pl.BlockSpec((FFN_CONV, tn), lambda i, j: (0, j)),
                  pl.BlockSpec((FFN_CONV, tn), lambda i, j: (0, nj + j)),
                  pl.BlockSpec((1, tn), lambda i, j: (0, j)),
                  pl.BlockSpec((1, tn), lambda i, j: (0, nj + j))],
        out_specs=pl.BlockSpec((tm, tn), lambda i, j: (i, j)),
        out_shape=jax.ShapeDtypeStruct((t, dff), BF16),
        scratch_shapes=[pltpu.VMEM((tm + halo, tn), F32)] * 2,
        compiler_params=_params(("parallel", "parallel"), block_bytes),
        name="ffn_up",
    )(xn, xn, w_up, w_up, conv_w, conv_w, conv_b.reshape(1, 2 * dff), conv_b.reshape(1, 2 * dff))


def kernel(x, positions, norm_mix_pre, w_in, attn_sinks, dn_conv_w, dn_a_log, dn_dt_bias, dn_norm_w,
           w_branch_attn, w_branch_dn, w_out, norm_mix_post, norm_ffn_pre, w_ffn_up, ffn_conv_w,
           ffn_conv_b, w_ffn_down, norm_ffn_post):
    b, seq, d = x.shape
    depth = w_in.shape[0]
    t = b * seq
    n_attn = attn_sinks.shape[1]
    n_dn = dn_a_log.shape[1]
    qa_w = n_attn * HEAD_DIM
    kv_w = ATTN_KV_HEADS * HEAD_DIM
    dn_w = n_dn * HEAD_DIM

    edges = [0]
    for wdt in (qa_w, kv_w, kv_w, 3 * dn_w, n_dn, n_dn, dn_w, d, d):
        edges.append(edges[-1] + wdt)
    main_cols = [(edges[0], edges[4]), (edges[6], edges[9])]
    w_main = jnp.concatenate([w_in[:, :, lo:hi] for lo, hi in main_cols], axis=-1).astype(BF16)
    w_gate = jnp.concatenate(
        [w_in[:, :, edges[4]:edges[5]]] + [w_in[:, :, edges[5]:edges[6]]] * (GATE_GROUPS - 1)
        + [jnp.zeros((depth, d, V7X_LANES - GATE_GROUPS * n_dn), w_in.dtype)], axis=-1).astype(BF16)
    q_off, k_off, v_off = 0, qa_w, qa_w + kv_w
    dq_off = qa_w + 2 * kv_w
    dk_off, dv_off, z_off = dq_off + dn_w, dq_off + 2 * dn_w, dq_off + 3 * dn_w
    ga_off, gd_off = z_off + dn_w, z_off + dn_w + d
    main_w = gd_off + d

    w_ba = w_branch_attn.astype(BF16)
    w_bd = w_branch_dn.astype(BF16)
    w_o = w_out.astype(BF16)
    w_up = w_ffn_up.astype(BF16)
    w_dn = w_ffn_down.astype(BF16)

    cos, sin = rope_tables(positions)
    cos = cos.reshape(b, seq, HEAD_DIM)
    sin = sin.reshape(b, seq, HEAD_DIM)

    h = x.reshape(t, d)
    xn = rmsnorm(h, norm_mix_pre[0])
    for l in range(depth):
        proj = matmul(xn, w_main[l])
        proj3 = proj.reshape(b, seq, main_w)
        o_a = attention(proj3, cos, sin, attn_sinks[l], seq=seq, q_off=q_off, k_off=k_off,
                        v_off=v_off, n_heads=n_attn)
        gt, gr = dn_gates(xn, w_gate[l], dn_a_log[l], dn_dt_bias[l], n_heads=n_dn)
        o_d = delta_net(proj3, gt.reshape(b, seq, V7X_LANES),
                        gr.reshape(b, seq // (DN_PACK * DN_CHUNK), n_dn, DN_PACK * DN_CHUNK),
                        dn_conv_w[l], dn_norm_w[l],
                        seq=seq, q_off=dq_off, k_off=dk_off, v_off=dv_off, z_off=z_off, n_heads=n_dn)
        y = branch_merge(o_a.reshape(t, qa_w), o_d.reshape(t, dn_w), proj, w_ba[l], w_bd[l],
                         ga_off=ga_off, gd_off=gd_off)
        h, xf = proj_norm_residual(y, w_o[l], h, norm_mix_post[l], norm_ffn_pre[l], tk=d)
        u = ffn_up(xf, w_up[l], ffn_conv_w[l], ffn_conv_b[l], seq=seq)
        next_w = norm_mix_pre[l + 1] if l + 1 < depth else None
        h, xn = proj_norm_residual(u, w_dn[l], h, norm_ffn_post[l], next_w, tk=1408)
    return h.reshape(b, seq, d)
```

```python
import functools

import jax
import jax.numpy as jnp
from jax import lax
from jax.experimental import pallas as pl
from jax.experimental.pallas import tpu as pltpu

HEAD_DIM = 128
ATTN_KV_HEADS = 4
ATTN_BLOCK = 128
ROT_DIM = HEAD_DIM // 4
ROPE_THETA = 500000.0
DN_CHUNK = 64
DN_CONV = 4
FFN_CONV = 3
EPS = 1e-6
LOG2E = 1.4426950408889634

V7X_LANES = 128
V7X_SUBLANES = 8
V7X_VMEM_BYTES = 64 * 1024 * 1024

DN_PACK = 2 * V7X_LANES // DN_CHUNK
DN_HEADS_PER_STEP = 8
INV_PASSES = 1
INV_BASE = 16

F32 = jnp.float32
BF16 = jnp.bfloat16


def _dot(a, b):
    return jnp.dot(a, b, preferred_element_type=F32)


def _dot_nt(a, b):
    return lax.dot_general(a, b, (((1,), (1,)), ((), ())), preferred_element_type=F32)


def _dot_tn(a, b):
    return lax.dot_general(a, b, (((0,), (0,)), ((), ())), preferred_element_type=F32)


def _rms(x, w):
    return x * lax.rsqrt(jnp.mean(x * x, axis=-1, keepdims=True) + EPS) * w


def _silu(x):
    h = 0.5 * x
    return h + h * jnp.tanh(h)


def _vmem_limit(block_bytes):
    return int(min(V7X_VMEM_BYTES - (2 << 20), 2 * block_bytes + (12 << 20)))


def _params(semantics, block_bytes):
    return pltpu.CompilerParams(dimension_semantics=semantics,
                                vmem_limit_bytes=_vmem_limit(block_bytes))


def _tile(n, want, align=V7X_LANES):
    for t in range(min(n, want) // align * align, 0, -align):
        if n % t == 0:
            return t
    raise ValueError(f"no {align}-aligned tile divides {n}")


def _rmsnorm_kernel(x_ref, w_ref, o_ref):
    o_ref[...] = _rms(x_ref[...], w_ref[...]).astype(o_ref.dtype)


def rmsnorm(x, w):
    t, d = x.shape
    tm = _tile(t, 512)
    return pl.pallas_call(
        _rmsnorm_kernel,
        grid=(t // tm,),
        in_specs=[pl.BlockSpec((tm, d), lambda i: (i, 0)),
                  pl.BlockSpec((1, d), lambda i: (0, 0))],
        out_specs=pl.BlockSpec((tm, d), lambda i: (i, 0)),
        out_shape=jax.ShapeDtypeStruct((t, d), BF16),
        compiler_params=_params(("parallel",), tm * d * 6),
        name="rmsnorm",
    )(x, w.reshape(1, d))


def _matmul_kernel(x_ref, w_ref, o_ref):
    o_ref[...] = _dot(x_ref[...], w_ref[...]).astype(o_ref.dtype)


def matmul(x, w):
    t, k = x.shape
    n = w.shape[1]
    tm = _tile(t, 1024)
    tn = _tile(n, 1024)
    return pl.pallas_call(
        _matmul_kernel,
        grid=(t // tm, n // tn),
        in_specs=[pl.BlockSpec((tm, k), lambda i, j: (i, 0)),
                  pl.BlockSpec((k, tn), lambda i, j: (0, j))],
        out_specs=pl.BlockSpec((tm, tn), lambda i, j: (i, j)),
        out_shape=jax.ShapeDtypeStruct((t, n), BF16),
        compiler_params=_params(("parallel", "parallel"), 2 * (tm * k + k * tn + tm * tn)),
        name="in_proj",
    )(x, w)


def _rope_table_kernel(pos_ref, freq_ref, sign_ref, cos_ref, sin_ref):
    ang = pos_ref[...] * freq_ref[...]
    cos_ref[...] = jnp.cos(ang)
    sin_ref[...] = jnp.sin(ang) * sign_ref[...]


def rope_tables(positions):
    t = positions.size
    half = ROT_DIM // 2
    inv_freq = ROPE_THETA ** (-jnp.arange(0, ROT_DIM, 2, dtype=F32) / ROT_DIM)
    freq = jnp.concatenate([inv_freq, inv_freq, jnp.zeros((HEAD_DIM - ROT_DIM,), F32)]).reshape(1, HEAD_DIM)
    sign = jnp.concatenate([-jnp.ones((half,), F32), jnp.ones((HEAD_DIM - half,), F32)]).reshape(1, HEAD_DIM)
    pos = positions.astype(F32).reshape(t, 1)
    tm = _tile(t, 1024)
    row = pl.BlockSpec((1, HEAD_DIM), lambda i: (0, 0))
    out = pl.BlockSpec((tm, HEAD_DIM), lambda i: (i, 0))
    return pl.pallas_call(
        _rope_table_kernel,
        grid=(t // tm,),
        in_specs=[pl.BlockSpec((tm, 1), lambda i: (i, 0)), row, row],
        out_specs=[out, out],
        out_shape=[jax.ShapeDtypeStruct((t, HEAD_DIM), F32)] * 2,
        compiler_params=_params(("parallel",), 3 * tm * HEAD_DIM * 4),
        name="rope_tables",
    )(pos, freq, sign)


def _rotate_half_matrix():
    half = ROT_DIM // 2
    r = jnp.arange(HEAD_DIM)[:, None]
    c = jnp.arange(HEAD_DIM)[None, :]
    p = ((c < half) & (r == c + half)) | ((c >= half) & (c < ROT_DIM) & (r == c - half))
    return p.astype(BF16)


def _attn_kernel(sink_ref, q_ref, k_ref, v_ref, cos_ref, sin_ref, perm_ref, o_ref, kbuf, vbuf,
                 *, tq, n_kv, group):
    i = pl.program_id(1)
    blk = ATTN_BLOCK
    hd = HEAD_DIM

    @pl.when(i == 0)
    def _():
        kbuf[0:blk, :] = jnp.zeros((blk, n_kv * hd), kbuf.dtype)
        vbuf[0:blk, :] = jnp.zeros((blk, n_kv * hd), vbuf.dtype)

    @pl.when(i > 0)
    def _():
        kbuf[0:blk, :] = kbuf[tq:tq + blk, :]
        vbuf[0:blk, :] = vbuf[tq:tq + blk, :]

    perm = perm_ref[...]

    def rope(xb, c, s):
        return (xb.astype(F32) * c + _dot(xb, perm) * s).astype(BF16)

    cos = cos_ref[...]
    sin = sin_ref[...]
    for kh in range(n_kv):
        cols = slice(kh * hd, (kh + 1) * hd)
        kbuf[blk:blk + tq, cols] = rope(k_ref[:, cols], cos, sin)
    vbuf[blk:blk + tq, :] = v_ref[...]

    r = lax.broadcasted_iota(jnp.int32, (group * blk, 2 * blk), 0) % blk
    c = lax.broadcasted_iota(jnp.int32, (group * blk, 2 * blk), 1)
    band = (c > r) & (c <= r + blk)
    band_first = band & (c >= blk)
    scale = hd ** -0.5 * LOG2E
    kv_heads = range(n_kv)

    def head_ids(kh):
        return [kh * group + g for g in range(group)]

    for j in range(tq // blk):
        rows = slice(j * blk, (j + 1) * blk)
        keys = slice(j * blk, (j + 2) * blk)
        cj = cos[rows]
        sj = sin[rows]
        mask = band if j > 0 else (band_first | (band & (i > 0)))
        qs = [jnp.concatenate([rope(q_ref[rows, h * hd:(h + 1) * hd], cj, sj) for h in head_ids(kh)], axis=0)
              for kh in kv_heads]
        logits = [jnp.where(mask, _dot_nt(qs[kh], kbuf[keys, kh * hd:(kh + 1) * hd]) * scale, -jnp.inf)
                  for kh in kv_heads]
        sinks = [jnp.concatenate([jnp.full((blk, 1), sink_ref[h] * LOG2E, F32) for h in head_ids(kh)], axis=0)
                 for kh in kv_heads]
        ms = [jnp.maximum(jnp.max(logits[kh], axis=-1, keepdims=True), sinks[kh]) for kh in kv_heads]
        ps = [jnp.exp2(logits[kh] - ms[kh]) for kh in kv_heads]
        denoms = [jnp.sum(ps[kh], axis=-1, keepdims=True) + jnp.exp2(sinks[kh] - ms[kh]) for kh in kv_heads]
        outs = [_dot(ps[kh].astype(BF16), vbuf[keys, kh * hd:(kh + 1) * hd]) / denoms[kh] for kh in kv_heads]
        for kh in kv_heads:
            for g, h in enumerate(head_ids(kh)):
                o_ref[rows, h * hd:(h + 1) * hd] = outs[kh][g * blk:(g + 1) * blk].astype(o_ref.dtype)


def attention(proj, cos, sin, sinks, *, seq, q_off, k_off, v_off, n_heads):
    b = proj.shape[0]
    n_kv = ATTN_KV_HEADS
    group = n_heads // n_kv
    qw = n_heads * HEAD_DIM
    kw = n_kv * HEAD_DIM
    tq = _tile(seq, 512)
    assert q_off % qw == 0 and k_off % kw == 0 and v_off % kw == 0
    kern = functools.partial(_attn_kernel, tq=tq, n_kv=n_kv, group=group)
    block_bytes = tq * (2 * qw + 2 * kw) * 2 + 2 * tq * HEAD_DIM * 4 + 2 * (tq + ATTN_BLOCK) * kw * 2
    return pl.pallas_call(
        kern,
        grid=(b, seq // tq),
        in_specs=[pl.BlockSpec(memory_space=pltpu.SMEM),
                  pl.BlockSpec((None, tq, qw), lambda bi, i: (bi, i, q_off // qw)),
                  pl.BlockSpec((None, tq, kw), lambda bi, i: (bi, i, k_off // kw)),
                  pl.BlockSpec((None, tq, kw), lambda bi, i: (bi, i, v_off // kw)),
                  pl.BlockSpec((None, tq, HEAD_DIM), lambda bi, i: (bi, i, 0)),
                  pl.BlockSpec((None, tq, HEAD_DIM), lambda bi, i: (bi, i, 0)),
                  pl.BlockSpec((HEAD_DIM, HEAD_DIM), lambda bi, i: (0, 0))],
        out_specs=pl.BlockSpec((None, tq, qw), lambda bi, i: (bi, i, 0)),
        out_shape=jax.ShapeDtypeStruct((b, seq, qw), BF16),
        scratch_shapes=[pltpu.VMEM((tq + ATTN_BLOCK, kw), BF16),
                        pltpu.VMEM((tq + ATTN_BLOCK, kw), BF16)],
        compiler_params=_params(("parallel", "arbitrary"), block_bytes),
        name="swa_attention",
    )(sinks, proj, proj, proj, cos, sin, _rotate_half_matrix())


GATE_GROUPS = 5


def _gates_kernel(x_ref, w_ref, alog_ref, dt_ref, gt_ref, gr_ref, *, n_heads, chunk):
    tm = x_ref.shape[0]
    pw = DN_PACK * chunk
    val = _dot(x_ref[...], w_ref[...])
    lane = lax.broadcasted_iota(jnp.int32, val.shape, 1)
    beta = jax.nn.sigmoid(val)
    z = val + dt_ref[...]
    softplus = jnp.maximum(z, 0.0) + jnp.log1p(jnp.exp(-jnp.abs(z)))
    g = jnp.where((lane >= n_heads) & (lane < GATE_GROUPS * n_heads), -jnp.exp(alog_ref[...]) * softplus, 0.0)
    r = lax.broadcasted_iota(jnp.int32, (tm, tm), 0)
    c = lax.broadcasted_iota(jnp.int32, (tm, tm), 1)
    same = (r // chunk) == (c // chunk)
    tri = (same & (c <= r)).astype(BF16)
    ones = same.astype(BF16)
    hi = g.astype(BF16)
    rem = g - hi.astype(F32)
    mid = rem.astype(BF16)
    lo = (rem - mid.astype(F32)).astype(BF16)
    gc = _dot(tri, hi) + _dot(tri, mid) + _dot(tri, lo)
    total = _dot(ones, hi) + _dot(ones, mid) + _dot(ones, lo)
    group = lane // n_heads
    table = jnp.where(group == 0, beta,
            jnp.where(group == 1, gc,
            jnp.where(group == 2, jnp.exp(gc),
            jnp.where(group == 3, jnp.exp(total - gc),
            jnp.where(group == 4, jnp.exp(total), 0.0)))))
    gt_ref[...] = table
    gct = gc.T
    for pi in range(tm // pw):
        gr_ref[pi] = gct[n_heads:2 * n_heads, pi * pw:(pi + 1) * pw]


def dn_gates(xn, w_gate, a_log, dt_bias, *, n_heads):
    t, d = xn.shape
    pw = DN_PACK * DN_CHUNK
    tm = _tile(t, 512)
    assert tm % pw == 0 and GATE_GROUPS * n_heads <= V7X_LANES
    pad = jnp.zeros((V7X_LANES - GATE_GROUPS * n_heads,), F32)
    zero = jnp.zeros((n_heads,), F32)
    alog = jnp.concatenate([zero] + [a_log] * (GATE_GROUPS - 1) + [pad]).reshape(1, V7X_LANES)
    dt = jnp.concatenate([zero] + [dt_bias] * (GATE_GROUPS - 1) + [pad]).reshape(1, V7X_LANES)
    row = pl.BlockSpec((1, V7X_LANES), lambda i: (0, 0))
    kern = functools.partial(_gates_kernel, n_heads=n_heads, chunk=DN_CHUNK)
    return pl.pallas_call(
        kern,
        grid=(t // tm,),
        in_specs=[pl.BlockSpec((tm, d), lambda i: (i, 0)),
                  pl.BlockSpec((d, V7X_LANES), lambda i: (0, 0)), row, row],
        out_specs=[pl.BlockSpec((tm, V7X_LANES), lambda i: (i, 0)),
                   pl.BlockSpec((tm // pw, n_heads, pw), lambda i: (i, 0, 0))],
        out_shape=[jax.ShapeDtypeStruct((t, V7X_LANES), F32),
                   jax.ShapeDtypeStruct((t // pw, n_heads, pw), F32)],
        compiler_params=_params(("parallel",), tm * d * 2 + d * V7X_LANES * 2 + 2 * tm * V7X_LANES * 4
                                + 4 * tm * tm),
        name="dn_gates",
    )(xn, w_gate, alog, dt)


def _block_diag(xp):
    n, m = xp.shape
    tiled = jnp.concatenate([xp] * (m // n), axis=0)
    r = lax.broadcasted_iota(jnp.int32, (m, m), 0) // n
    c = lax.broadcasted_iota(jnp.int32, (m, m), 1) // n
    return jnp.where(r == c, tiled, jnp.zeros_like(tiled))


def _packed_dot(xp, yp, passes):
    xh = xp.astype(BF16)
    yh = yp.astype(BF16)
    out = _dot(xh, _block_diag(yh))
    if passes >= 3:
        xl = (xp - xh.astype(F32)).astype(BF16)
        yl = (yp - yh.astype(F32)).astype(BF16)
        out = out + _dot(xh, _block_diag(yl)) + _dot(xl, _block_diag(yh))
    return out


def _packed_unit_lower_inverses(aps, passes):
    n = aps[0].shape[0]
    r = lax.broadcasted_iota(jnp.int32, aps[0].shape, 0)
    c = lax.broadcasted_iota(jnp.int32, aps[0].shape, 1) % n
    eye = (r == c).astype(F32)
    size = INV_BASE
    in_blk = (r // size) == (c // size)
    xs = [jnp.where(in_blk, -ap, 0.0) for ap in aps]
    invs = [eye + x for x in xs]
    power = 1
    while 2 * power < size:
        xs = [_packed_dot(x, x, passes) for x in xs]
        invs = [inv + _packed_dot(inv, x, passes) for inv, x in zip(invs, xs)]
        power *= 2
    while size < n:
        merged = (r // (2 * size)) == (c // (2 * size))
        offs = [_packed_dot(jnp.where(merged & ~in_blk, ap, 0.0), inv, passes) for ap, inv in zip(aps, invs)]
        invs = [inv - _packed_dot(inv, off, passes) for inv, off in zip(invs, offs)]
        in_blk = merged
        size *= 2
    return invs


def _dn_kernel(q_ref, k_ref, v_ref, z_ref, wq_ref, wk_ref, wv_ref, gt_ref, gr_ref, nw_ref, o_ref,
               qtail, ktail, vtail, qn, kn, vn, rhs_s, u_s, wq_s, kd_s, qkm_s, vnew_s, qs_s, state,
               *, ts, hg, chunk):
    s = pl.program_id(2)
    gi = pl.program_id(1)
    hd = HEAD_DIM
    halo = V7X_SUBLANES

    @pl.when(s == 0)
    def _():
        state[...] = jnp.zeros(state.shape, F32)
        for tail in (qtail, ktail, vtail):
            tail[...] = jnp.zeros(tail.shape, F32)

    def conv_silu(src_ref, tail, w_ref):
        x = src_ref[...].astype(F32)
        xe = jnp.concatenate([tail[...], x], axis=0)
        tail[...] = x[ts - halo:, :]
        return _silu(_causal_conv(xe, w_ref[...])[halo:])

    def l2norm(x, scale):
        return x * (lax.rsqrt(jnp.sum(x * x, axis=-1, keepdims=True) + EPS) * scale)

    qc = conv_silu(q_ref, qtail, wq_ref)
    kc = conv_silu(k_ref, ktail, wk_ref)
    vn[...] = conv_silu(v_ref, vtail, wv_ref)
    for hh in range(hg):
        cols = slice(hh * hd, (hh + 1) * hd)
        qn[:, cols] = l2norm(qc[:, cols], hd ** -0.5)
        kn[:, cols] = l2norm(kc[:, cols], 1.0)

    pack = DN_PACK
    pw = pack * chunk
    n_pack = ts // pw
    n_heads = gr_ref.shape[1]
    r = lax.broadcasted_iota(jnp.int32, (pw, pw), 0)
    c = lax.broadcasted_iota(jnp.int32, (pw, pw), 1)
    same = (r // chunk) == (c // chunk)
    tri = same & (r >= c)
    strict = same & (r > c)
    lane = lax.broadcasted_iota(jnp.int32, (pw, V7X_LANES), 1)
    nw = nw_ref[...]

    jobs = [(hh, p) for hh in range(hg) for p in range(n_pack)]
    chunk_decay = {}
    aps = []
    for hh, p in jobs:
        cols = slice(hh * hd, (hh + 1) * hd)
        rows = slice(p * pw, (p + 1) * pw)
        h = gi * hg + hh
        gt = gt_ref[rows, :]

        def gate_col(group, gt=gt, h=h):
            return jnp.sum(jnp.where(lane == group * n_heads + h, gt, 0.0), axis=-1, keepdims=True)

        beta, gc, eg, erem, etot = (gate_col(i) for i in range(GATE_GROUPS))
        gc_row = gr_ref[p, pl.ds(h, 1), :]
        q = qn[rows, cols]
        k = kn[rows, cols]
        v = vn[rows, cols]
        kb = k.astype(BF16)
        sims = _dot_nt(jnp.concatenate([kb, q.astype(BF16)], axis=0), kb)
        decay = jnp.exp(jnp.where(tri, gc - gc_row, -jnp.inf))
        a_full = jnp.where(strict, beta * sims[0:pw] * decay, 0.0)
        ap = a_full[0:chunk]
        for bb in range(1, pack):
            ap = ap + a_full[bb * chunk:(bb + 1) * chunk]
        aps.append(ap)
        rhs_s[hh, rows, :] = jnp.concatenate([v * beta, k * (beta * eg)], axis=1).astype(BF16)
        qg = (q * eg).astype(BF16)
        for bb in range(pack):
            wq_s[hh, p * pack + bb, chunk:2 * chunk, :] = qg[bb * chunk:(bb + 1) * chunk]
            chunk_decay[(hh, p * pack + bb)] = etot[bb * chunk:bb * chunk + 1, :]
        kd_s[hh, rows, :] = (k * erem).astype(BF16)
        qkm_s[hh, p] = (sims[pw:2 * pw] * decay).astype(BF16)

    tps = _packed_unit_lower_inverses(aps, INV_PASSES)

    for (hh, p), tp in zip(jobs, tps):
        rows = slice(p * pw, (p + 1) * pw)
        uw = _dot(_block_diag(tp.astype(BF16)), rhs_s[hh, rows, :])
        u_s[hh, rows, :] = uw[:, 0:hd]
        w = uw[:, hd:2 * hd].astype(BF16)
        for bb in range(pack):
            wq_s[hh, p * pack + bb, 0:chunk, :] = w[bb * chunk:(bb + 1) * chunk]

    for cidx in range(ts // chunk):
        rows = slice(cidx * chunk, (cidx + 1) * chunk)
        for hh in range(hg):
            st = state[hh]
            ws_qs = _dot(wq_s[hh, cidx], st.astype(BF16))
            v_new = (u_s[hh, rows, :] - ws_qs[0:chunk]).astype(BF16)
            vnew_s[hh, rows, :] = v_new
            qs_s[hh, rows, :] = ws_qs[chunk:2 * chunk]
            state[hh] = st * chunk_decay[(hh, cidx)] + _dot_tn(kd_s[hh, rows, :], v_new)

    for hh in range(hg):
        cols = slice(hh * hd, (hh + 1) * hd)
        for p in range(n_pack):
            rows = slice(p * pw, (p + 1) * pw)
            o = qs_s[hh, rows, :] + _dot(qkm_s[hh, p], vnew_s[hh, rows, :])
            zz = z_ref[rows, cols].astype(F32)
            o = o * lax.rsqrt(jnp.mean(o * o, axis=-1, keepdims=True) + EPS) * nw * _silu(zz)
            o_ref[rows, cols] = o.astype(o_ref.dtype)


def delta_net(proj, gt, gr, conv_w, norm_w, *, seq, q_off, k_off, v_off, z_off, n_heads):
    b = proj.shape[0]
    hd = HEAD_DIM
    hg = min(DN_HEADS_PER_STEP, n_heads)
    pw = DN_PACK * DN_CHUNK
    ts = _tile(seq, 512, align=pw)
    gw = hg * hd
    kw = n_heads * hd
    assert n_heads % hg == 0
    for off in (q_off, k_off, v_off, z_off, kw):
        assert off % gw == 0
    kern = functools.partial(_dn_kernel, ts=ts, hg=hg, chunk=DN_CHUNK)

    def col_spec(off):
        return pl.BlockSpec((None, ts, gw), lambda bi, g, s: (bi, s, off // gw + g))

    def w_spec(off):
        return pl.BlockSpec((DN_CONV, gw), lambda bi, g, s: (0, off // gw + g))

    scratch = ([pltpu.VMEM((V7X_SUBLANES, gw), F32)] * 3
               + [pltpu.VMEM((ts, gw), F32)] * 3
               + [pltpu.VMEM((hg, ts, 2 * hd), BF16),
                  pltpu.VMEM((hg, ts, hd), F32),
                  pltpu.VMEM((hg, ts // DN_CHUNK, 2 * DN_CHUNK, hd), BF16),
                  pltpu.VMEM((hg, ts, hd), BF16),
                  pltpu.VMEM((hg, ts // pw, pw, pw), BF16),
                  pltpu.VMEM((hg, ts, hd), BF16),
                  pltpu.VMEM((hg, ts, hd), F32),
                  pltpu.VMEM((hg, hd, hd), F32)])
    block_bytes = (5 * ts * gw * 2 + ts * V7X_LANES * 4 + (ts // pw) * n_heads * pw * 4
                   + 3 * (ts + V7X_SUBLANES) * gw * 4 + 3 * ts * gw * 4 + hg * ts * hd * 18)
    return pl.pallas_call(
        kern,
        grid=(b, n_heads // hg, seq // ts),
        in_specs=[col_spec(q_off), col_spec(k_off), col_spec(v_off), col_spec(z_off),
                  w_spec(0), w_spec(kw), w_spec(2 * kw),
                  pl.BlockSpec((None, ts, V7X_LANES), lambda bi, g, s: (bi, s, 0)),
                  pl.BlockSpec((None, ts // pw, n_heads, pw), lambda bi, g, s: (bi, s, 0, 0)),
                  pl.BlockSpec((1, hd), lambda bi, g, s: (0, 0))],
        out_specs=pl.BlockSpec((None, ts, gw), lambda bi, g, s: (bi, s, g)),
        out_shape=jax.ShapeDtypeStruct((b, seq, kw), BF16),
        scratch_shapes=scratch,
        compiler_params=_params(("parallel", "parallel", "arbitrary"), block_bytes),
        name="delta_net",
    )(proj, proj, proj, proj, conv_w, conv_w, conv_w, gt, gr, norm_w.reshape(1, hd))


def _merge_kernel(oa_ref, od_ref, ga_ref, gd_ref, wa_ref, wd_ref, y_ref):
    a = _dot(oa_ref[...], wa_ref[...])
    d = _dot(od_ref[...], wd_ref[...])
    y = jax.nn.sigmoid(ga_ref[...].astype(F32)) * a + jax.nn.sigmoid(gd_ref[...].astype(F32)) * d
    y_ref[...] = y.astype(y_ref.dtype)


def branch_merge(o_a, o_d, proj, w_a, w_d, *, ga_off, gd_off):
    t, ka = o_a.shape
    kd = o_d.shape[1]
    n = w_a.shape[1]
    tm = _tile(t, 1024)
    tn = _tile(n, 512)
    assert ga_off % tn == 0 and gd_off % tn == 0
    block_bytes = 2 * (tm * ka + tm * kd + 3 * tm * tn + ka * tn + kd * tn)
    return pl.pallas_call(
        _merge_kernel,
        grid=(t // tm, n // tn),
        in_specs=[pl.BlockSpec((tm, ka), lambda i, j: (i, 0)),
                  pl.BlockSpec((tm, kd), lambda i, j: (i, 0)),
                  pl.BlockSpec((tm, tn), lambda i, j: (i, ga_off // tn + j)),
                  pl.BlockSpec((tm, tn), lambda i, j: (i, gd_off // tn + j)),
                  pl.BlockSpec((ka, tn), lambda i, j: (0, j)),
                  pl.BlockSpec((kd, tn), lambda i, j: (0, j))],
        out_specs=pl.BlockSpec((tm, tn), lambda i, j: (i, j)),
        out_shape=jax.ShapeDtypeStruct((t, n), BF16),
        compiler_params=_params(("parallel", "parallel"), block_bytes),
        name="branch_merge",
    )(o_a, o_d, proj, proj, w_a, w_d)


def _proj_norm_res_kernel(*refs, nk, with_next):
    if with_next:
        x_ref, w_ref, h_ref, post_ref, next_ref, h_out, xn_out = refs[:7]
        rest = refs[7:]
    else:
        x_ref, w_ref, h_ref, post_ref, h_out = refs[:5]
        next_ref = xn_out = None
        rest = refs[5:]

    def epilogue(f):
        hn = h_ref[...] + _rms(f, post_ref[...])
        h_out[...] = hn
        if with_next:
            xn_out[...] = _rms(hn, next_ref[...]).astype(xn_out.dtype)

    if nk == 1:
        epilogue(_dot(x_ref[...], w_ref[...]))
        return
    acc = rest[0]
    kk = pl.program_id(1)

    @pl.when(kk == 0)
    def _():
        acc[...] = _dot(x_ref[...], w_ref[...])

    @pl.when((kk > 0) & (kk < nk - 1))
    def _():
        acc[...] += _dot(x_ref[...], w_ref[...])

    @pl.when(kk == nk - 1)
    def _():
        epilogue(acc[...] + _dot(x_ref[...], w_ref[...]))


def proj_norm_residual(x, w, h, post_w, next_w, *, tk):
    t, k = x.shape
    d = w.shape[1]
    tm = _tile(t, 512)
    tk = _tile(k, tk)
    nk = k // tk
    with_next = next_w is not None
    kern = functools.partial(_proj_norm_res_kernel, nk=nk, with_next=with_next)
    row = pl.BlockSpec((1, d), lambda i, kk: (0, 0))
    tile = pl.BlockSpec((tm, d), lambda i, kk: (i, 0))
    in_specs = [pl.BlockSpec((tm, tk), lambda i, kk: (i, kk)),
                pl.BlockSpec((tk, d), lambda i, kk: (kk, 0)), tile, row]
    args = [x, w, h, post_w.reshape(1, d)]
    out_specs = [tile]
    out_shape = [jax.ShapeDtypeStruct((t, d), F32)]
    if with_next:
        in_specs.append(row)
        args.append(next_w.reshape(1, d))
        out_specs.append(tile)
        out_shape.append(jax.ShapeDtypeStruct((t, d), BF16))
    block_bytes = tm * tk * 2 + tk * d * 2 + tm * d * (4 + 4 + 2) + (tm * d * 2 if nk > 1 else 0)
    res = pl.pallas_call(
        kern,
        grid=(t // tm, nk),
        in_specs=in_specs,
        out_specs=out_specs,
        out_shape=out_shape,
        scratch_shapes=[pltpu.VMEM((tm, d), F32)] if nk > 1 else [],
        compiler_params=_params(("parallel", "arbitrary"), block_bytes),
        name="proj_norm_residual",
    )(*args)
    return (res[0], res[1]) if with_next else (res[0], None)


def _causal_conv(u, taps):
    y = u * taps[0:1, :]
    for j in range(1, taps.shape[0]):
        y = pltpu.roll(y, 1, 0) + u * taps[j:j + 1, :]
    return y


def _ffn_up_kernel(x_ref, wg_ref, wv_ref, cg_ref, cv_ref, bg_ref, bv_ref, o_ref, tail_g, tail_v,
                   *, tiles_per_seq, sub):
    i = pl.program_id(1)
    halo = V7X_SUBLANES
    tm, tn = o_ref.shape
    x = x_ref[...]
    blocks = [slice(lo, min(lo + sub, tn)) for lo in range(0, tn, sub)]

    @pl.when(i == 0)
    def _():
        tail_g[...] = jnp.zeros(tail_g.shape, F32)
        tail_v[...] = jnp.zeros(tail_v.shape, F32)

    def project(cols):
        return _dot(x, wg_ref[:, cols]), _dot(x, wv_ref[:, cols])

    def conv(u, tail, cols, c_ref, b_ref):
        hist = jnp.where((i % tiles_per_seq) != 0, tail[:, cols], 0.0)
        tail[:, cols] = u[tm - halo:, :]
        return _causal_conv(jnp.concatenate([hist, u], axis=0), c_ref[:, cols])[halo:] + b_ref[:, cols]

    pending = project(blocks[0])
    for s, cols in enumerate(blocks):
        ug, uv = pending
        if s + 1 < len(blocks):
            pending = project(blocks[s + 1])
        gate = conv(ug, tail_g, cols, cg_ref, bg_ref)
        val = conv(uv, tail_v, cols, cv_ref, bv_ref)
        o_ref[:, cols] = (_silu(gate) * val).astype(o_ref.dtype)


def ffn_up(xn, w_up, conv_w, conv_b, *, seq):
    t, d = xn.shape
    dff = w_up.shape[1] // 2
    tm = _tile(seq, 1024)
    sub = 2 * V7X_LANES
    tn = _tile(dff, 2816)
    halo = V7X_SUBLANES
    nj = dff // tn
    kern = functools.partial(_ffn_up_kernel, tiles_per_seq=seq // tm, sub=sub)

    def w_spec(off):
        return pl.BlockSpec((d, tn), lambda j, i: (0, off + j), pipeline_mode=pl.Buffered(1))

    block_bytes = tm * d * 2 + tm * tn * 2 + d * tn * 2 + 2 * (tm + halo) * sub * 4
    return pl.pallas_call(
        kern,
        grid=(nj, t // tm),
        in_specs=[pl.BlockSpec((tm, d), lambda j, i: (i, 0)),
                  w_spec(0), w_spec(nj),
                  pl.BlockSpec((FFN_CONV, tn), lambda j, i: (0, j)),
                  pl.BlockSpec((FFN_CONV, tn), lambda j, i: (0, nj + j)),
                  pl.BlockSpec((1, tn), lambda j, i: (0, j)),
                  pl.BlockSpec((1, tn), lambda j, i: (0, nj + j))],
        out_specs=pl.BlockSpec((tm, tn), lambda j, i: (i, j)),
        out_shape=jax.ShapeDtypeStruct((t, dff), BF16),
        scratch_shapes=[pltpu.VMEM((halo, tn), F32)] * 2,
        compiler_params=_params(("arbitrary", "arbitrary"), block_bytes),
        name="ffn_up",
    )(xn, w_up, w_up, conv_w, conv_w, conv_b.reshape(1, 2 * dff), conv_b.reshape(1, 2 * dff))


def kernel(x, positions, norm_mix_pre, w_in, attn_sinks, dn_conv_w, dn_a_log, dn_dt_bias, dn_norm_w,
           w_branch_attn, w_branch_dn, w_out, norm_mix_post, norm_ffn_pre, w_ffn_up, ffn_conv_w,
           ffn_conv_b, w_ffn_down, norm_ffn_post):
    b, seq, d = x.shape
    depth = w_in.shape[0]
    t = b * seq
    n_attn = attn_sinks.shape[1]
    n_dn = dn_a_log.shape[1]
    qa_w = n_attn * HEAD_DIM
    kv_w = ATTN_KV_HEADS * HEAD_DIM
    dn_w = n_dn * HEAD_DIM

    edges = [0]
    for wdt in (qa_w, kv_w, kv_w, 3 * dn_w, n_dn, n_dn, dn_w, d, d):
        edges.append(edges[-1] + wdt)
    main_cols = [(edges[0], edges[4]), (edges[6], edges[9])]
    w_main = jnp.concatenate([w_in[:, :, lo:hi] for lo, hi in main_cols], axis=-1).astype(BF16)
    w_gate = jnp.concatenate(
        [w_in[:, :, edges[4]:edges[5]]] + [w_in[:, :, edges[5]:edges[6]]] * (GATE_GROUPS - 1)
        + [jnp.zeros((depth, d, V7X_LANES - GATE_GROUPS * n_dn), w_in.dtype)], axis=-1).astype(BF16)
    q_off, k_off, v_off = 0, qa_w, qa_w + kv_w
    dq_off = qa_w + 2 * kv_w
    dk_off, dv_off, z_off = dq_off + dn_w, dq_off + 2 * dn_w, dq_off + 3 * dn_w
    ga_off, gd_off = z_off + dn_w, z_off + dn_w + d
    main_w = gd_off + d

    w_ba = w_branch_attn.astype(BF16)
    w_bd = w_branch_dn.astype(BF16)
    w_o = w_out.astype(BF16)
    w_up = w_ffn_up.astype(BF16)
    w_dn = w_ffn_down.astype(BF16)

    cos, sin = rope_tables(positions)
    cos = cos.reshape(b, seq, HEAD_DIM)
    sin = sin.reshape(b, seq, HEAD_DIM)

    h = x.reshape(t, d)
    xn = rmsnorm(h, norm_mix_pre[0])
    for l in range(depth):
        proj = matmul(xn, w_main[l])
        proj3 = proj.reshape(b, seq, main_w)
        o_a = attention(proj3, cos, sin, attn_sinks[l], seq=seq, q_off=q_off, k_off=k_off,
                        v_off=v_off, n_heads=n_attn)
        gt, gr = dn_gates(xn, w_gate[l], dn_a_log[l], dn_dt_bias[l], n_heads=n_dn)
        o_d = delta_net(proj3, gt.reshape(b, seq, V7X_LANES),
                        gr.reshape(b, seq // (DN_PACK * DN_CHUNK), n_dn, DN_PACK * DN_CHUNK),
                        dn_conv_w[l], dn_norm_w[l],
                        seq=seq, q_off=dq_off, k_off=dk_off, v_off=dv_off, z_off=z_off, n_heads=n_dn)
        y = branch_merge(o_a.reshape(t, qa_w), o_d.reshape(t, dn_w), proj, w_ba[l], w_bd[l],
                         ga_off=ga_off, gd_off=gd_off)
        h, xf = proj_norm_residual(y, w_o[l], h, norm_mix_post[l], norm_ffn_pre[l], tk=d)
        u = ffn_up(xf, w_up[l], ffn_conv_w[l], ffn_conv_b[l], seq=seq)
        next_w = norm_mix_pre[l + 1] if l + 1 < depth else None
        h, xn = proj_norm_residual(u, w_dn[l], h, norm_ffn_post[l], next_w, tk=1408)
    return h.reshape(b, seq, d)
```

```python
import functools

import jax
import jax.numpy as jnp
from jax import lax
from jax.experimental import pallas as pl
from jax.experimental.pallas import tpu as pltpu

HEAD_DIM = 128
ATTN_KV_HEADS = 4
ATTN_BLOCK = 128
ROT_DIM = HEAD_DIM // 4
ROPE_THETA = 500000.0
DN_CHUNK = 64
DN_CONV = 4
FFN_CONV = 3
EPS = 1e-6
LOG2E = 1.4426950408889634

V7X_LANES = 128
V7X_SUBLANES = 8
V7X_VMEM_BYTES = 64 * 1024 * 1024

DN_PACK = 2 * V7X_LANES // DN_CHUNK
DN_HEADS_PER_STEP = 8
INV_PASSES = 1
INV_BASE = 16

F32 = jnp.float32
BF16 = jnp.bfloat16


def _dot(a, b):
    return jnp.dot(a, b, preferred_element_type=F32)


def _dot_nt(a, b):
    return lax.dot_general(a, b, (((1,), (1,)), ((), ())), preferred_element_type=F32)


def _dot_tn(a, b):
    return lax.dot_general(a, b, (((0,), (0,)), ((), ())), preferred_element_type=F32)


def _rms(x, w):
    return x * lax.rsqrt(jnp.mean(x * x, axis=-1, keepdims=True) + EPS) * w


def _silu(x):
    h = 0.5 * x
    return h + h * jnp.tanh(h)


def _vmem_limit(block_bytes):
    return int(min(V7X_VMEM_BYTES - (2 << 20), 2 * block_bytes + (12 << 20)))


def _params(semantics, block_bytes):
    return pltpu.CompilerParams(dimension_semantics=semantics,
                                vmem_limit_bytes=_vmem_limit(block_bytes))


def _tile(n, want, align=V7X_LANES):
    for t in range(min(n, want) // align * align, 0, -align):
        if n % t == 0:
            return t
    raise ValueError(f"no {align}-aligned tile divides {n}")


def _rmsnorm_kernel(x_ref, w_ref, o_ref):
    o_ref[...] = _rms(x_ref[...], w_ref[...]).astype(o_ref.dtype)


def rmsnorm(x, w):
    t, d = x.shape
    tm = _tile(t, 512)
    return pl.pallas_call(
        _rmsnorm_kernel,
        grid=(t // tm,),
        in_specs=[pl.BlockSpec((tm, d), lambda i: (i, 0)),
                  pl.BlockSpec((1, d), lambda i: (0, 0))],
        out_specs=pl.BlockSpec((tm, d), lambda i: (i, 0)),
        out_shape=jax.ShapeDtypeStruct((t, d), BF16),
        compiler_params=_params(("parallel",), tm * d * 6),
        name="rmsnorm",
    )(x, w.reshape(1, d))


def _matmul_kernel(x_ref, w_ref, o_ref):
    o_ref[...] = _dot(x_ref[...], w_ref[...]).astype(o_ref.dtype)


def matmul(x, w):
    t, k = x.shape
    n = w.shape[1]
    tm = _tile(t, 1024)
    tn = _tile(n, 1024)
    return pl.pallas_call(
        _matmul_kernel,
        grid=(t // tm, n // tn),
        in_specs=[pl.BlockSpec((tm, k), lambda i, j: (i, 0)),
                  pl.BlockSpec((k, tn), lambda i, j: (0, j))],
        out_specs=pl.BlockSpec((tm, tn), lambda i, j: (i, j)),
        out_shape=jax.ShapeDtypeStruct((t, n), BF16),
        compiler_params=_params(("parallel", "parallel"), 2 * (tm * k + k * tn + tm * tn)),
        name="in_proj",
    )(x, w)


def _rope_table_kernel(pos_ref, freq_ref, sign_ref, cos_ref, sin_ref):
    ang = pos_ref[...] * freq_ref[...]
    cos_ref[...] = jnp.cos(ang)
    sin_ref[...] = jnp.sin(ang) * sign_ref[...]


def rope_tables(positions):
    t = positions.size
    half = ROT_DIM // 2
    inv_freq = ROPE_THETA ** (-jnp.arange(0, ROT_DIM, 2, dtype=F32) / ROT_DIM)
    freq = jnp.concatenate([inv_freq, inv_freq, jnp.zeros((HEAD_DIM - ROT_DIM,), F32)]).reshape(1, HEAD_DIM)
    sign = jnp.concatenate([-jnp.ones((half,), F32), jnp.ones((HEAD_DIM - half,), F32)]).reshape(1, HEAD_DIM)
    pos = positions.astype(F32).reshape(t, 1)
    tm = _tile(t, 1024)
    row = pl.BlockSpec((1, HEAD_DIM), lambda i: (0, 0))
    out = pl.BlockSpec((tm, HEAD_DIM), lambda i: (i, 0))
    return pl.pallas_call(
        _rope_table_kernel,
        grid=(t // tm,),
        in_specs=[pl.BlockSpec((tm, 1), lambda i: (i, 0)), row, row],
        out_specs=[out, out],
        out_shape=[jax.ShapeDtypeStruct((t, HEAD_DIM), F32)] * 2,
        compiler_params=_params(("parallel",), 3 * tm * HEAD_DIM * 4),
        name="rope_tables",
    )(pos, freq, sign)


def _rotate_half_matrix():
    half = ROT_DIM // 2
    r = jnp.arange(HEAD_DIM)[:, None]
    c = jnp.arange(HEAD_DIM)[None, :]
    p = ((c < half) & (r == c + half)) | ((c >= half) & (c < ROT_DIM) & (r == c - half))
    return p.astype(BF16)


def _attn_kernel(sink_ref, q_ref, k_ref, v_ref, cos_ref, sin_ref, perm_ref, o_ref, kbuf, vbuf,
                 *, tq, n_kv, group):
    i = pl.program_id(1)
    blk = ATTN_BLOCK
    hd = HEAD_DIM

    @pl.when(i == 0)
    def _():
        kbuf[0:blk, :] = jnp.zeros((blk, n_kv * hd), kbuf.dtype)
        vbuf[0:blk, :] = jnp.zeros((blk, n_kv * hd), vbuf.dtype)

    @pl.when(i > 0)
    def _():
        kbuf[0:blk, :] = kbuf[tq:tq + blk, :]
        vbuf[0:blk, :] = vbuf[tq:tq + blk, :]

    perm = perm_ref[...]

    def rope(xb, c, s):
        return (xb.astype(F32) * c + _dot(xb, perm) * s).astype(BF16)

    cos = cos_ref[...]
    sin = sin_ref[...]
    for kh in range(n_kv):
        cols = slice(kh * hd, (kh + 1) * hd)
        kbuf[blk:blk + tq, cols] = rope(k_ref[:, cols], cos, sin)
    vbuf[blk:blk + tq, :] = v_ref[...]

    r = lax.broadcasted_iota(jnp.int32, (group * blk, 2 * blk), 0) % blk
    c = lax.broadcasted_iota(jnp.int32, (group * blk, 2 * blk), 1)
    band = (c > r) & (c <= r + blk)
    band_first = band & (c >= blk)
    scale = hd ** -0.5 * LOG2E
    kv_heads = range(n_kv)

    def head_ids(kh):
        return [kh * group + g for g in range(group)]

    for j in range(tq // blk):
        rows = slice(j * blk, (j + 1) * blk)
        keys = slice(j * blk, (j + 2) * blk)
        cj = cos[rows]
        sj = sin[rows]
        mask = band if j > 0 else (band_first | (band & (i > 0)))
        qs = [jnp.concatenate([rope(q_ref[rows, h * hd:(h + 1) * hd], cj, sj) for h in head_ids(kh)], axis=0)
              for kh in kv_heads]
        logits = [jnp.where(mask, _dot_nt(qs[kh], kbuf[keys, kh * hd:(kh + 1) * hd]) * scale, -jnp.inf)
                  for kh in kv_heads]
        sinks = [jnp.concatenate([jnp.full((blk, 1), sink_ref[h] * LOG2E, F32) for h in head_ids(kh)], axis=0)
                 for kh in kv_heads]
        ms = [jnp.maximum(jnp.max(logits[kh], axis=-1, keepdims=True), sinks[kh]) for kh in kv_heads]
        ps = [jnp.exp2(logits[kh] - ms[kh]) for kh in kv_heads]
        denoms = [jnp.sum(ps[kh], axis=-1, keepdims=True) + jnp.exp2(sinks[kh] - ms[kh]) for kh in kv_heads]
        outs = [_dot(ps[kh].astype(BF16), vbuf[keys, kh * hd:(kh + 1) * hd]) / denoms[kh] for kh in kv_heads]
        for kh in kv_heads:
            for g, h in enumerate(head_ids(kh)):
                o_ref[rows, h * hd:(h + 1) * hd] = outs[kh][g * blk:(g + 1) * blk].astype(o_ref.dtype)


def attention(proj, cos, sin, sinks, *, seq, q_off, k_off, v_off, n_heads):
    b = proj.shape[0]
    n_kv = ATTN_KV_HEADS
    group = n_heads // n_kv
    qw = n_heads * HEAD_DIM
    kw = n_kv * HEAD_DIM
    tq = _tile(seq, 512)
    assert q_off % qw == 0 and k_off % kw == 0 and v_off % kw == 0
    kern = functools.partial(_attn_kernel, tq=tq, n_kv=n_kv, group=group)
    block_bytes = tq * (2 * qw + 2 * kw) * 2 + 2 * tq * HEAD_DIM * 4 + 2 * (tq + ATTN_BLOCK) * kw * 2
    return pl.pallas_call(
        kern,
        grid=(b, seq // tq),
        in_specs=[pl.BlockSpec(memory_space=pltpu.SMEM),
                  pl.BlockSpec((None, tq, qw), lambda bi, i: (bi, i, q_off // qw)),
                  pl.BlockSpec((None, tq, kw), lambda bi, i: (bi, i, k_off // kw)),
                  pl.BlockSpec((None, tq, kw), lambda bi, i: (bi, i, v_off // kw)),
                  pl.BlockSpec((None, tq, HEAD_DIM), lambda bi, i: (bi, i, 0)),
                  pl.BlockSpec((None, tq, HEAD_DIM), lambda bi, i: (bi, i, 0)),
                  pl.BlockSpec((HEAD_DIM, HEAD_DIM), lambda bi, i: (0, 0))],
        out_specs=pl.BlockSpec((None, tq, qw), lambda bi, i: (bi, i, 0)),
        out_shape=jax.ShapeDtypeStruct((b, seq, qw), BF16),
        scratch_shapes=[pltpu.VMEM((tq + ATTN_BLOCK, kw), BF16),
                        pltpu.VMEM((tq + ATTN_BLOCK, kw), BF16)],
        compiler_params=_params(("parallel", "arbitrary"), block_bytes),
        name="swa_attention",
    )(sinks, proj, proj, proj, cos, sin, _rotate_half_matrix())


GATE_GROUPS = 5


def _gate_lanes(b_part, a_part, hg):
    n_heads = a_part.shape[-1]
    assert n_heads % hg == 0 and GATE_GROUPS * hg <= V7X_LANES
    pad = jnp.zeros(a_part.shape[:-1] + (V7X_LANES - GATE_GROUPS * hg,), a_part.dtype)
    parts = []
    for g in range(n_heads // hg):
        heads = slice(g * hg, (g + 1) * hg)
        parts += [b_part[..., heads]] + [a_part[..., heads]] * (GATE_GROUPS - 1) + [pad]
    return jnp.concatenate(parts, axis=-1)


def _gates_kernel(x_ref, w_ref, alog_ref, dt_ref, gt_ref, gr_ref, *, hg, chunk):
    tm = x_ref.shape[0]
    pw = DN_PACK * chunk
    val = _dot(x_ref[...], w_ref[...])
    quantity = (lax.broadcasted_iota(jnp.int32, val.shape, 1) % V7X_LANES) // hg
    beta = jax.nn.sigmoid(val)
    z = val + dt_ref[...]
    softplus = jnp.maximum(z, 0.0) + jnp.log1p(jnp.exp(-jnp.abs(z)))
    g = jnp.where((quantity >= 1) & (quantity < GATE_GROUPS), -jnp.exp(alog_ref[...]) * softplus, 0.0)
    r = lax.broadcasted_iota(jnp.int32, (tm, tm), 0)
    c = lax.broadcasted_iota(jnp.int32, (tm, tm), 1)
    tri = (((r // chunk) == (c // chunk)) & (c <= r)).astype(BF16)
    hi = g.astype(BF16)
    rem = g - hi.astype(F32)
    mid = rem.astype(BF16)
    lo = (rem - mid.astype(F32)).astype(BF16)
    gc = _dot(tri, hi) + _dot(tri, mid) + _dot(tri, lo)
    total = jnp.concatenate(
        [jnp.broadcast_to(gc[(ci + 1) * chunk - 1:(ci + 1) * chunk, :], (chunk, gc.shape[1]))
         for ci in range(tm // chunk)], axis=0)
    gt_ref[...] = jnp.where(quantity == 0, beta,
                  jnp.where(quantity == 1, gc,
                  jnp.where(quantity == 2, jnp.exp(gc),
                  jnp.where(quantity == 3, jnp.exp(total - gc),
                  jnp.where(quantity == 4, jnp.exp(total), 0.0)))))
    gct = gc.T
    for pi in range(tm // pw):
        for grp in range(gc.shape[1] // V7X_LANES):
            lo_row = grp * V7X_LANES + hg
            gr_ref[pi, grp * hg:(grp + 1) * hg, :] = gct[lo_row:lo_row + hg, pi * pw:(pi + 1) * pw]


def dn_gates(xn, w_gate, a_log, dt_bias, *, hg):
    t, d = xn.shape
    n_heads = a_log.shape[0]
    gl = w_gate.shape[1]
    pw = DN_PACK * DN_CHUNK
    tm = _tile(t, 512)
    assert tm % pw == 0
    alog = _gate_lanes(jnp.zeros_like(a_log), a_log, hg).reshape(1, gl)
    dt = _gate_lanes(jnp.zeros_like(dt_bias), dt_bias, hg).reshape(1, gl)
    row = pl.BlockSpec((1, gl), lambda i: (0, 0))
    kern = functools.partial(_gates_kernel, hg=hg, chunk=DN_CHUNK)
    return pl.pallas_call(
        kern,
        grid=(t // tm,),
        in_specs=[pl.BlockSpec((tm, d), lambda i: (i, 0)),
                  pl.BlockSpec((d, gl), lambda i: (0, 0)), row, row],
        out_specs=[pl.BlockSpec((tm, gl), lambda i: (i, 0)),
                   pl.BlockSpec((tm // pw, n_heads, pw), lambda i: (i, 0, 0))],
        out_shape=[jax.ShapeDtypeStruct((t, gl), F32),
                   jax.ShapeDtypeStruct((t // pw, n_heads, pw), F32)],
        compiler_params=_params(("parallel",), tm * d * 2 + d * gl * 2 + 4 * tm * gl * 4 + 4 * tm * tm),
        name="dn_gates",
    )(xn, w_gate, alog, dt)


def _block_diag(xp):
    n, m = xp.shape
    tiled = jnp.concatenate([xp] * (m // n), axis=0)
    r = lax.broadcasted_iota(jnp.int32, (m, m), 0) // n
    c = lax.broadcasted_iota(jnp.int32, (m, m), 1) // n
    return jnp.where(r == c, tiled, jnp.zeros_like(tiled))


def _packed_dot(xp, yp, passes):
    xh = xp.astype(BF16)
    yh = yp.astype(BF16)
    out = _dot(xh, _block_diag(yh))
    if passes >= 3:
        xl = (xp - xh.astype(F32)).astype(BF16)
        yl = (yp - yh.astype(F32)).astype(BF16)
        out = out + _dot(xh, _block_diag(yl)) + _dot(xl, _block_diag(yh))
    return out


def _packed_unit_lower_inverses(aps, passes):
    n = aps[0].shape[0]
    r = lax.broadcasted_iota(jnp.int32, aps[0].shape, 0)
    c = lax.broadcasted_iota(jnp.int32, aps[0].shape, 1) % n
    eye = (r == c).astype(F32)
    size = INV_BASE
    in_blk = (r // size) == (c // size)
    xs = [jnp.where(in_blk, -ap, 0.0) for ap in aps]
    invs = [eye + x for x in xs]
    power = 1
    while 2 * power < size:
        xs = [_packed_dot(x, x, passes) for x in xs]
        invs = [inv + _packed_dot(inv, x, passes) for inv, x in zip(invs, xs)]
        power *= 2
    while size < n:
        merged = (r // (2 * size)) == (c // (2 * size))
        offs = [_packed_dot(jnp.where(merged & ~in_blk, ap, 0.0), inv, passes) for ap, inv in zip(aps, invs)]
        invs = [inv - _packed_dot(inv, off, passes) for inv, off in zip(invs, offs)]
        in_blk = merged
        size *= 2
    return invs


def _dn_kernel(q_ref, k_ref, v_ref, z_ref, wq_ref, wk_ref, wv_ref, gt_ref, gr_ref, nw_ref, o_ref,
               qtail, ktail, vtail, qn, kn, vn, rhs_s, u_s, wq_s, kd_s, qkm_s, vnew_s, qs_s, state,
               *, ts, hg, chunk):
    s = pl.program_id(2)
    gi = pl.program_id(1)
    hd = HEAD_DIM
    halo = V7X_SUBLANES

    @pl.when(s == 0)
    def _():
        state[...] = jnp.zeros(state.shape, F32)
        for tail in (qtail, ktail, vtail):
            tail[...] = jnp.zeros(tail.shape, F32)

    def conv_silu(src_ref, tail, w_ref):
        x = src_ref[...].astype(F32)
        xe = jnp.concatenate([tail[...], x], axis=0)
        tail[...] = x[ts - halo:, :]
        return _silu(_causal_conv(xe, w_ref[...])[halo:])

    def l2norm(x, scale):
        return x * (lax.rsqrt(jnp.sum(x * x, axis=-1, keepdims=True) + EPS) * scale)

    qc = conv_silu(q_ref, qtail, wq_ref)
    kc = conv_silu(k_ref, ktail, wk_ref)
    vn[...] = conv_silu(v_ref, vtail, wv_ref)
    for hh in range(hg):
        cols = slice(hh * hd, (hh + 1) * hd)
        qn[:, cols] = l2norm(qc[:, cols], hd ** -0.5)
        kn[:, cols] = l2norm(kc[:, cols], 1.0)

    pack = DN_PACK
    pw = pack * chunk
    n_pack = ts // pw
    r = lax.broadcasted_iota(jnp.int32, (pw, pw), 0)
    c = lax.broadcasted_iota(jnp.int32, (pw, pw), 1)
    same = (r // chunk) == (c // chunk)
    tri = same & (r >= c)
    strict = same & (r > c)
    nw = nw_ref[...]

    jobs = [(hh, p) for hh in range(hg) for p in range(n_pack)]
    chunk_decay = {}
    aps = []
    for hh, p in jobs:
        cols = slice(hh * hd, (hh + 1) * hd)
        rows = slice(p * pw, (p + 1) * pw)
        beta, gc, eg, erem, etot = (gt_ref[rows, qi * hg + hh:qi * hg + hh + 1] for qi in range(GATE_GROUPS))
        gc_row = gr_ref[p, pl.ds(gi * hg + hh, 1), :]
        q = qn[rows, cols]
        k = kn[rows, cols]
        v = vn[rows, cols]
        kb = k.astype(BF16)
        sims = _dot_nt(jnp.concatenate([kb, q.astype(BF16)], axis=0), kb)
        decay = jnp.exp(jnp.where(tri, gc - gc_row, -jnp.inf))
        a_full = jnp.where(strict, beta * sims[0:pw] * decay, 0.0)
        ap = a_full[0:chunk]
        for bb in range(1, pack):
            ap = ap + a_full[bb * chunk:(bb + 1) * chunk]
        aps.append(ap)
        rhs_s[hh, rows, :] = jnp.concatenate([v * beta, k * (beta * eg)], axis=1).astype(BF16)
        qg = (q * eg).astype(BF16)
        for bb in range(pack):
            wq_s[hh, p * pack + bb, chunk:2 * chunk, :] = qg[bb * chunk:(bb + 1) * chunk]
            chunk_decay[(hh, p * pack + bb)] = etot[bb * chunk:bb * chunk + 1, :]
        kd_s[hh, rows, :] = (k * erem).astype(BF16)
        qkm_s[hh, p] = (sims[pw:2 * pw] * decay).astype(BF16)

    tps = _packed_unit_lower_inverses(aps, INV_PASSES)

    for (hh, p), tp in zip(jobs, tps):
        rows = slice(p * pw, (p + 1) * pw)
        uw = _dot(_block_diag(tp.astype(BF16)), rhs_s[hh, rows, :])
        u_s[hh, rows, :] = uw[:, 0:hd]
        w = uw[:, hd:2 * hd].astype(BF16)
        for bb in range(pack):
            wq_s[hh, p * pack + bb, 0:chunk, :] = w[bb * chunk:(bb + 1) * chunk]

    for cidx in range(ts // chunk):
        rows = slice(cidx * chunk, (cidx + 1) * chunk)
        for hh in range(hg):
            st = state[hh]
            ws_qs = _dot(wq_s[hh, cidx], st.astype(BF16))
            v_new = (u_s[hh, rows, :] - ws_qs[0:chunk]).astype(BF16)
            vnew_s[hh, rows, :] = v_new
            qs_s[hh, rows, :] = ws_qs[chunk:2 * chunk]
            state[hh] = st * chunk_decay[(hh, cidx)] + _dot_tn(kd_s[hh, rows, :], v_new)

    for hh in range(hg):
        cols = slice(hh * hd, (hh + 1) * hd)
        for p in range(n_pack):
            rows = slice(p * pw, (p + 1) * pw)
            o = qs_s[hh, rows, :] + _dot(qkm_s[hh, p], vnew_s[hh, rows, :])
            zz = z_ref[rows, cols].astype(F32)
            o = o * lax.rsqrt(jnp.mean(o * o, axis=-1, keepdims=True) + EPS) * nw * _silu(zz)
            o_ref[rows, cols] = o.astype(o_ref.dtype)


def delta_net(proj, gt, gr, conv_w, norm_w, *, seq, q_off, k_off, v_off, z_off, n_heads):
    b = proj.shape[0]
    hd = HEAD_DIM
    hg = min(DN_HEADS_PER_STEP, n_heads)
    pw = DN_PACK * DN_CHUNK
    ts = _tile(seq, 512, align=pw)
    gw = hg * hd
    kw = n_heads * hd
    assert n_heads % hg == 0
    for off in (q_off, k_off, v_off, z_off, kw):
        assert off % gw == 0
    kern = functools.partial(_dn_kernel, ts=ts, hg=hg, chunk=DN_CHUNK)

    def col_spec(off):
        return pl.BlockSpec((None, ts, gw), lambda bi, g, s: (bi, s, off // gw + g))

    def w_spec(off):
        return pl.BlockSpec((DN_CONV, gw), lambda bi, g, s: (0, off // gw + g))

    scratch = ([pltpu.VMEM((V7X_SUBLANES, gw), F32)] * 3
               + [pltpu.VMEM((ts, gw), F32)] * 3
               + [pltpu.VMEM((hg, ts, 2 * hd), BF16),
                  pltpu.VMEM((hg, ts, hd), F32),
                  pltpu.VMEM((hg, ts // DN_CHUNK, 2 * DN_CHUNK, hd), BF16),
                  pltpu.VMEM((hg, ts, hd), BF16),
                  pltpu.VMEM((hg, ts // pw, pw, pw), BF16),
                  pltpu.VMEM((hg, ts, hd), BF16),
                  pltpu.VMEM((hg, ts, hd), F32),
                  pltpu.VMEM((hg, hd, hd), F32)])
    block_bytes = (5 * ts * gw * 2 + ts * V7X_LANES * 4 + (ts // pw) * n_heads * pw * 4
                   + 3 * (ts + V7X_SUBLANES) * gw * 4 + 3 * ts * gw * 4 + hg * ts * hd * 18)
    return pl.pallas_call(
        kern,
        grid=(b, n_heads // hg, seq // ts),
        in_specs=[col_spec(q_off), col_spec(k_off), col_spec(v_off), col_spec(z_off),
                  w_spec(0), w_spec(kw), w_spec(2 * kw),
                  pl.BlockSpec((None, ts, V7X_LANES), lambda bi, g, s: (bi, s, g)),
                  pl.BlockSpec((None, ts // pw, n_heads, pw), lambda bi, g, s: (bi, s, 0, 0)),
                  pl.BlockSpec((1, hd), lambda bi, g, s: (0, 0))],
        out_specs=pl.BlockSpec((None, ts, gw), lambda bi, g, s: (bi, s, g)),
        out_shape=jax.ShapeDtypeStruct((b, seq, kw), BF16),
        scratch_shapes=scratch,
        compiler_params=_params(("parallel", "parallel", "arbitrary"), block_bytes),
        name="delta_net",
    )(proj, proj, proj, proj, conv_w, conv_w, conv_w, gt, gr, norm_w.reshape(1, hd))


def _merge_kernel(oa_ref, od_ref, ga_ref, gd_ref, wa_ref, wd_ref, y_ref):
    a = _dot(oa_ref[...], wa_ref[...])
    d = _dot(od_ref[...], wd_ref[...])
    y = jax.nn.sigmoid(ga_ref[...].astype(F32)) * a + jax.nn.sigmoid(gd_ref[...].astype(F32)) * d
    y_ref[...] = y.astype(y_ref.dtype)


def branch_merge(o_a, o_d, proj, w_a, w_d, *, ga_off, gd_off):
    t, ka = o_a.shape
    kd = o_d.shape[1]
    n = w_a.shape[1]
    tm = _tile(t, 1024)
    tn = _tile(n, 512)
    assert ga_off % tn == 0 and gd_off % tn == 0
    block_bytes = 2 * (tm * ka + tm * kd + 3 * tm * tn + ka * tn + kd * tn)
    return pl.pallas_call(
        _merge_kernel,
        grid=(t // tm, n // tn),
        in_specs=[pl.BlockSpec((tm, ka), lambda i, j: (i, 0)),
                  pl.BlockSpec((tm, kd), lambda i, j: (i, 0)),
                  pl.BlockSpec((tm, tn), lambda i, j: (i, ga_off // tn + j)),
                  pl.BlockSpec((tm, tn), lambda i, j: (i, gd_off // tn + j)),
                  pl.BlockSpec((ka, tn), lambda i, j: (0, j)),
                  pl.BlockSpec((kd, tn), lambda i, j: (0, j))],
        out_specs=pl.BlockSpec((tm, tn), lambda i, j: (i, j)),
        out_shape=jax.ShapeDtypeStruct((t, n), BF16),
        compiler_params=_params(("parallel", "parallel"), block_bytes),
        name="branch_merge",
    )(o_a, o_d, proj, proj, w_a, w_d)


def _proj_norm_res_kernel(*refs, with_next):
    if with_next:
        x_ref, w_ref, h_ref, post_ref, next_ref, h_out, xn_out = refs
    else:
        x_ref, w_ref, h_ref, post_ref, h_out = refs
    hn = h_ref[...] + _rms(_dot(x_ref[...], w_ref[...]), post_ref[...])
    h_out[...] = hn
    if with_next:
        xn_out[...] = _rms(hn, next_ref[...]).astype(xn_out.dtype)


def proj_norm_residual(x, w, h, post_w, next_w):
    t, k = x.shape
    d = w.shape[1]
    tm = _tile(t, 256)
    with_next = next_w is not None
    kern = functools.partial(_proj_norm_res_kernel, with_next=with_next)
    row = pl.BlockSpec((1, d), lambda i: (0, 0))
    tile = pl.BlockSpec((tm, d), lambda i: (i, 0))
    in_specs = [pl.BlockSpec((tm, k), lambda i: (i, 0)),
                pl.BlockSpec((k, d), lambda i: (0, 0), pipeline_mode=pl.Buffered(1)), tile, row]
    args = [x, w, h, post_w.reshape(1, d)]
    out_specs = [tile]
    out_shape = [jax.ShapeDtypeStruct((t, d), F32)]
    if with_next:
        in_specs.append(row)
        args.append(next_w.reshape(1, d))
        out_specs.append(tile)
        out_shape.append(jax.ShapeDtypeStruct((t, d), BF16))
    block_bytes = tm * k * 2 + tm * d * (4 + 4 + 2) + (k * d * 2) // 2 + tm * d * 4
    res = pl.pallas_call(
        kern,
        grid=(t // tm,),
        in_specs=in_specs,
        out_specs=out_specs,
        out_shape=out_shape,
        compiler_params=pltpu.CompilerParams(
            dimension_semantics=("parallel",),
            vmem_limit_bytes=int(min(V7X_VMEM_BYTES - (2 << 20), 2 * block_bytes + (4 << 20)))),
        name="proj_norm_residual",
    )(*args)
    return (res[0], res[1]) if with_next else (res[0], None)


def _causal_conv(u, taps):
    y = u * taps[0:1, :]
    for j in range(1, taps.shape[0]):
        y = pltpu.roll(y, 1, 0) + u * taps[j:j + 1, :]
    return y


def _ffn_up_kernel(x_ref, wg_ref, wv_ref, cg_ref, cv_ref, bg_ref, bv_ref, o_ref, tail_g, tail_v,
                   *, tiles_per_seq, sub):
    i = pl.program_id(1)
    halo = V7X_SUBLANES
    tm, tn = o_ref.shape
    x = x_ref[...]
    blocks = [slice(lo, min(lo + sub, tn)) for lo in range(0, tn, sub)]

    @pl.when(i == 0)
    def _():
        tail_g[...] = jnp.zeros(tail_g.shape, F32)
        tail_v[...] = jnp.zeros(tail_v.shape, F32)

    def project(cols):
        return _dot(x, wg_ref[:, cols]), _dot(x, wv_ref[:, cols])

    def conv(u, tail, cols, c_ref, b_ref):
        hist = jnp.where((i % tiles_per_seq) != 0, tail[:, cols], 0.0)
        tail[:, cols] = u[tm - halo:, :]
        return _causal_conv(jnp.concatenate([hist, u], axis=0), c_ref[:, cols])[halo:] + b_ref[:, cols]

    pending = project(blocks[0])
    for s, cols in enumerate(blocks):
        ug, uv = pending
        if s + 1 < len(blocks):
            pending = project(blocks[s + 1])
        gate = conv(ug, tail_g, cols, cg_ref, bg_ref)
        val = conv(uv, tail_v, cols, cv_ref, bv_ref)
        o_ref[:, cols] = (_silu(gate) * val).astype(o_ref.dtype)


def ffn_up(xn, w_up, conv_w, conv_b, *, seq):
    t, d = xn.shape
    dff = w_up.shape[1] // 2
    tm = _tile(seq, 1024)
    sub = 2 * V7X_LANES
    tn = _tile(dff, 2816)
    halo = V7X_SUBLANES
    nj = dff // tn
    kern = functools.partial(_ffn_up_kernel, tiles_per_seq=seq // tm, sub=sub)

    def w_spec(off):
        return pl.BlockSpec((d, tn), lambda j, i: (0, off + j), pipeline_mode=pl.Buffered(1))

    block_bytes = tm * d * 2 + tm * tn * 2 + d * tn * 2 + 2 * (tm + halo) * sub * 4
    return pl.pallas_call(
        kern,
        grid=(nj, t // tm),
        in_specs=[pl.BlockSpec((tm, d), lambda j, i: (i, 0)),
                  w_spec(0), w_spec(nj),
                  pl.BlockSpec((FFN_CONV, tn), lambda j, i: (0, j)),
                  pl.BlockSpec((FFN_CONV, tn), lambda j, i: (0, nj + j)),
                  pl.BlockSpec((1, tn), lambda j, i: (0, j)),
                  pl.BlockSpec((1, tn), lambda j, i: (0, nj + j))],
        out_specs=pl.BlockSpec((tm, tn), lambda j, i: (i, j)),
        out_shape=jax.ShapeDtypeStruct((t, dff), BF16),
        scratch_shapes=[pltpu.VMEM((halo, tn), F32)] * 2,
        compiler_params=_params(("arbitrary", "arbitrary"), block_bytes),
        name="ffn_up",
    )(xn, w_up, w_up, conv_w, conv_w, conv_b.reshape(1, 2 * dff), conv_b.reshape(1, 2 * dff))


def kernel(x, positions, norm_mix_pre, w_in, attn_sinks, dn_conv_w, dn_a_log, dn_dt_bias, dn_norm_w,
           w_branch_attn, w_branch_dn, w_out, norm_mix_post, norm_ffn_pre, w_ffn_up, ffn_conv_w,
           ffn_conv_b, w_ffn_down, norm_ffn_post):
    b, seq, d = x.shape
    depth = w_in.shape[0]
    t = b * seq
    n_attn = attn_sinks.shape[1]
    n_dn = dn_a_log.shape[1]
    qa_w = n_attn * HEAD_DIM
    kv_w = ATTN_KV_HEADS * HEAD_DIM
    dn_w = n_dn * HEAD_DIM

    edges = [0]
    for wdt in (qa_w, kv_w, kv_w, 3 * dn_w, n_dn, n_dn, dn_w, d, d):
        edges.append(edges[-1] + wdt)
    main_cols = [(edges[0], edges[4]), (edges[6], edges[9])]
    w_in_b = w_in.astype(BF16)
    w_main = jnp.concatenate([w_in_b[:, :, lo:hi] for lo, hi in main_cols], axis=-1)
    dn_hg = min(DN_HEADS_PER_STEP, n_dn)
    w_gate = _gate_lanes(w_in_b[:, :, edges[4]:edges[5]], w_in_b[:, :, edges[5]:edges[6]], dn_hg)
    q_off, k_off, v_off = 0, qa_w, qa_w + kv_w
    dq_off = qa_w + 2 * kv_w
    dk_off, dv_off, z_off = dq_off + dn_w, dq_off + 2 * dn_w, dq_off + 3 * dn_w
    ga_off, gd_off = z_off + dn_w, z_off + dn_w + d
    main_w = gd_off + d

    w_ba = w_branch_attn.astype(BF16)
    w_bd = w_branch_dn.astype(BF16)
    w_o = w_out.astype(BF16)
    w_up = w_ffn_up.astype(BF16)
    w_dn = w_ffn_down.astype(BF16)

    cos, sin = rope_tables(positions)
    cos = cos.reshape(b, seq, HEAD_DIM)
    sin = sin.reshape(b, seq, HEAD_DIM)

    h = x.reshape(t, d)
    xn = rmsnorm(h, norm_mix_pre[0])
    for l in range(depth):
        proj = matmul(xn, w_main[l])
        proj3 = proj.reshape(b, seq, main_w)
        o_a = attention(proj3, cos, sin, attn_sinks[l], seq=seq, q_off=q_off, k_off=k_off,
                        v_off=v_off, n_heads=n_attn)
        gt, gr = dn_gates(xn, w_gate[l], dn_a_log[l], dn_dt_bias[l], hg=dn_hg)
        o_d = delta_net(proj3, gt.reshape(b, seq, gt.shape[1]),
                        gr.reshape(b, seq // (DN_PACK * DN_CHUNK), n_dn, DN_PACK * DN_CHUNK),
                        dn_conv_w[l], dn_norm_w[l],
                        seq=seq, q_off=dq_off, k_off=dk_off, v_off=dv_off, z_off=z_off, n_heads=n_dn)
        y = branch_merge(o_a.reshape(t, qa_w), o_d.reshape(t, dn_w), proj, w_ba[l], w_bd[l],
                         ga_off=ga_off, gd_off=gd_off)
        h, xf = proj_norm_residual(y, w_o[l], h, norm_mix_post[l], norm_ffn_pre[l])
        u = ffn_up(xf, w_up[l], ffn_conv_w[l], ffn_conv_b[l], seq=seq)
        next_w = norm_mix_pre[l + 1] if l + 1 < depth else None
        h, xn = proj_norm_residual(u, w_dn[l], h, norm_ffn_post[l], next_w)
    return h.reshape(b, seq, d)
```

```python
import functools

import jax
import jax.numpy as jnp
from jax import lax
from jax.experimental import pallas as pl
from jax.experimental.pallas import tpu as pltpu

HEAD_DIM = 128
ATTN_KV_HEADS = 4
ATTN_BLOCK = 128
ROT_DIM = HEAD_DIM // 4
ROPE_THETA = 500000.0
DN_CHUNK = 64
DN_CONV = 4
FFN_CONV = 3
EPS = 1e-6
LOG2E = 1.4426950408889634

V7X_LANES = 128
V7X_SUBLANES = 8
V7X_VMEM_BYTES = 64 * 1024 * 1024

DN_PACK = 2 * V7X_LANES // DN_CHUNK
DN_HEADS_PER_STEP = 8
INV_PASSES = 1
INV_BASE = 16

F32 = jnp.float32
BF16 = jnp.bfloat16


def _dot(a, b):
    return jnp.dot(a, b, preferred_element_type=F32)


def _dot_nt(a, b):
    return lax.dot_general(a, b, (((1,), (1,)), ((), ())), preferred_element_type=F32)


def _dot_tn(a, b):
    return lax.dot_general(a, b, (((0,), (0,)), ((), ())), preferred_element_type=F32)


def _rms(x, w):
    return x * lax.rsqrt(jnp.mean(x * x, axis=-1, keepdims=True) + EPS) * w


def _silu(x):
    h = 0.5 * x
    return h + h * jnp.tanh(h)


def _vmem_limit(block_bytes):
    return int(min(V7X_VMEM_BYTES - (2 << 20), 2 * block_bytes + (12 << 20)))


def _params(semantics, block_bytes):
    return pltpu.CompilerParams(dimension_semantics=semantics,
                                vmem_limit_bytes=_vmem_limit(block_bytes))


def _tile(n, want, align=V7X_LANES):
    for t in range(min(n, want) // align * align, 0, -align):
        if n % t == 0:
            return t
    raise ValueError(f"no {align}-aligned tile divides {n}")


def _rmsnorm_kernel(x_ref, w_ref, o_ref):
    o_ref[...] = _rms(x_ref[...], w_ref[...]).astype(o_ref.dtype)


def rmsnorm(x, w):
    t, d = x.shape
    tm = _tile(t, 512)
    return pl.pallas_call(
        _rmsnorm_kernel,
        grid=(t // tm,),
        in_specs=[pl.BlockSpec((tm, d), lambda i: (i, 0)),
                  pl.BlockSpec((1, d), lambda i: (0, 0))],
        out_specs=pl.BlockSpec((tm, d), lambda i: (i, 0)),
        out_shape=jax.ShapeDtypeStruct((t, d), BF16),
        compiler_params=_params(("parallel",), tm * d * 6),
        name="rmsnorm",
    )(x, w.reshape(1, d))


def _matmul_kernel(x_ref, w_ref, o_ref):
    o_ref[...] = _dot(x_ref[...], w_ref[...]).astype(o_ref.dtype)


def matmul(x, w, layer, n):
    t, k = x.shape
    tm = _tile(t, 1024)
    tn = _tile(n, 1024)
    return pl.pallas_call(
        _matmul_kernel,
        grid=(t // tm, n // tn),
        in_specs=[pl.BlockSpec((tm, k), lambda i, j: (i, 0)),
                  pl.BlockSpec((None, k, tn), lambda i, j: (layer, 0, j))],
        out_specs=pl.BlockSpec((tm, tn), lambda i, j: (i, j)),
        out_shape=jax.ShapeDtypeStruct((t, n), BF16),
        compiler_params=_params(("parallel", "parallel"), 2 * (tm * k + k * tn + tm * tn)),
        name="in_proj",
    )(x, w)


def _rope_table_kernel(pos_ref, freq_ref, sign_ref, cos_ref, sin_ref):
    ang = pos_ref[...] * freq_ref[...]
    cos_ref[...] = jnp.cos(ang)
    sin_ref[...] = jnp.sin(ang) * sign_ref[...]


def rope_tables(positions):
    t = positions.size
    half = ROT_DIM // 2
    inv_freq = ROPE_THETA ** (-jnp.arange(0, ROT_DIM, 2, dtype=F32) / ROT_DIM)
    freq = jnp.concatenate([inv_freq, inv_freq, jnp.zeros((HEAD_DIM - ROT_DIM,), F32)]).reshape(1, HEAD_DIM)
    sign = jnp.concatenate([-jnp.ones((half,), F32), jnp.ones((HEAD_DIM - half,), F32)]).reshape(1, HEAD_DIM)
    pos = positions.astype(F32).reshape(t, 1)
    tm = _tile(t, 1024)
    row = pl.BlockSpec((1, HEAD_DIM), lambda i: (0, 0))
    out = pl.BlockSpec((tm, HEAD_DIM), lambda i: (i, 0))
    return pl.pallas_call(
        _rope_table_kernel,
        grid=(t // tm,),
        in_specs=[pl.BlockSpec((tm, 1), lambda i: (i, 0)), row, row],
        out_specs=[out, out],
        out_shape=[jax.ShapeDtypeStruct((t, HEAD_DIM), F32)] * 2,
        compiler_params=_params(("parallel",), 3 * tm * HEAD_DIM * 4),
        name="rope_tables",
    )(pos, freq, sign)


def _rotate_half_matrix():
    half = ROT_DIM // 2
    r = jnp.arange(HEAD_DIM)[:, None]
    c = jnp.arange(HEAD_DIM)[None, :]
    p = ((c < half) & (r == c + half)) | ((c >= half) & (c < ROT_DIM) & (r == c - half))
    return p.astype(BF16)


def _attn_kernel(sink_ref, q_ref, k_ref, v_ref, cos_ref, sin_ref, perm_ref, o_ref, kbuf, vbuf,
                 *, tq, n_kv, group):
    i = pl.program_id(1)
    blk = ATTN_BLOCK
    hd = HEAD_DIM

    @pl.when(i == 0)
    def _():
        kbuf[0:blk, :] = jnp.zeros((blk, n_kv * hd), kbuf.dtype)
        vbuf[0:blk, :] = jnp.zeros((blk, n_kv * hd), vbuf.dtype)

    @pl.when(i > 0)
    def _():
        kbuf[0:blk, :] = kbuf[tq:tq + blk, :]
        vbuf[0:blk, :] = vbuf[tq:tq + blk, :]

    perm = perm_ref[...]

    def rope(xb, c, s):
        return (xb.astype(F32) * c + _dot(xb, perm) * s).astype(BF16)

    cos = cos_ref[...]
    sin = sin_ref[...]
    for kh in range(n_kv):
        cols = slice(kh * hd, (kh + 1) * hd)
        kbuf[blk:blk + tq, cols] = rope(k_ref[:, cols], cos, sin)
    vbuf[blk:blk + tq, :] = v_ref[...]

    r = lax.broadcasted_iota(jnp.int32, (group * blk, 2 * blk), 0) % blk
    c = lax.broadcasted_iota(jnp.int32, (group * blk, 2 * blk), 1)
    band = (c > r) & (c <= r + blk)
    band_first = band & (c >= blk)
    scale = hd ** -0.5 * LOG2E
    kv_heads = range(n_kv)

    def head_ids(kh):
        return [kh * group + g for g in range(group)]

    for j in range(tq // blk):
        rows = slice(j * blk, (j + 1) * blk)
        keys = slice(j * blk, (j + 2) * blk)
        cj = cos[rows]
        sj = sin[rows]
        mask = band if j > 0 else (band_first | (band & (i > 0)))
        qs = [jnp.concatenate([rope(q_ref[rows, h * hd:(h + 1) * hd], cj, sj) for h in head_ids(kh)], axis=0)
              for kh in kv_heads]
        logits = [jnp.where(mask, _dot_nt(qs[kh], kbuf[keys, kh * hd:(kh + 1) * hd]) * scale, -jnp.inf)
                  for kh in kv_heads]
        sinks = [jnp.concatenate([jnp.full((blk, 1), sink_ref[h] * LOG2E, F32) for h in head_ids(kh)], axis=0)
                 for kh in kv_heads]
        ms = [jnp.maximum(jnp.max(logits[kh], axis=-1, keepdims=True), sinks[kh]) for kh in kv_heads]
        ps = [jnp.exp2(logits[kh] - ms[kh]) for kh in kv_heads]
        denoms = [jnp.sum(ps[kh], axis=-1, keepdims=True) + jnp.exp2(sinks[kh] - ms[kh]) for kh in kv_heads]
        outs = [_dot(ps[kh].astype(BF16), vbuf[keys, kh * hd:(kh + 1) * hd]) / denoms[kh] for kh in kv_heads]
        for kh in kv_heads:
            for g, h in enumerate(head_ids(kh)):
                o_ref[rows, h * hd:(h + 1) * hd] = outs[kh][g * blk:(g + 1) * blk].astype(o_ref.dtype)


def attention(proj, cos, sin, sinks, *, seq, q_off, k_off, v_off, n_heads):
    b = proj.shape[0]
    n_kv = ATTN_KV_HEADS
    group = n_heads // n_kv
    qw = n_heads * HEAD_DIM
    kw = n_kv * HEAD_DIM
    tq = _tile(seq, 512)
    assert q_off % qw == 0 and k_off % kw == 0 and v_off % kw == 0
    kern = functools.partial(_attn_kernel, tq=tq, n_kv=n_kv, group=group)
    block_bytes = tq * (2 * qw + 2 * kw) * 2 + 2 * tq * HEAD_DIM * 4 + 2 * (tq + ATTN_BLOCK) * kw * 2
    return pl.pallas_call(
        kern,
        grid=(b, seq // tq),
        in_specs=[pl.BlockSpec(memory_space=pltpu.SMEM),
                  pl.BlockSpec((None, tq, qw), lambda bi, i: (bi, i, q_off // qw)),
                  pl.BlockSpec((None, tq, kw), lambda bi, i: (bi, i, k_off // kw)),
                  pl.BlockSpec((None, tq, kw), lambda bi, i: (bi, i, v_off // kw)),
                  pl.BlockSpec((None, tq, HEAD_DIM), lambda bi, i: (bi, i, 0)),
                  pl.BlockSpec((None, tq, HEAD_DIM), lambda bi, i: (bi, i, 0)),
                  pl.BlockSpec((HEAD_DIM, HEAD_DIM), lambda bi, i: (0, 0))],
        out_specs=pl.BlockSpec((None, tq, qw), lambda bi, i: (bi, i, 0)),
        out_shape=jax.ShapeDtypeStruct((b, seq, qw), BF16),
        scratch_shapes=[pltpu.VMEM((tq + ATTN_BLOCK, kw), BF16),
                        pltpu.VMEM((tq + ATTN_BLOCK, kw), BF16)],
        compiler_params=_params(("parallel", "arbitrary"), block_bytes),
        name="swa_attention",
    )(sinks, proj, proj, proj, cos, sin, _rotate_half_matrix())


GATE_GROUPS = 5


def _gate_lanes(b_part, a_part, hg):
    n_heads = a_part.shape[-1]
    assert n_heads % hg == 0 and GATE_GROUPS * hg <= V7X_LANES
    pad = jnp.zeros(a_part.shape[:-1] + (V7X_LANES - GATE_GROUPS * hg,), a_part.dtype)
    parts = []
    for g in range(n_heads // hg):
        heads = slice(g * hg, (g + 1) * hg)
        parts += [b_part[..., heads]] + [a_part[..., heads]] * (GATE_GROUPS - 1) + [pad]
    return jnp.concatenate(parts, axis=-1)


def _gates_kernel(x_ref, w_ref, alog_ref, dt_ref, gt_ref, gr_ref, *, hg, chunk):
    tm = x_ref.shape[0]
    pw = DN_PACK * chunk
    val = _dot(x_ref[...], w_ref[...])
    quantity = (lax.broadcasted_iota(jnp.int32, val.shape, 1) % V7X_LANES) // hg
    beta = jax.nn.sigmoid(val)
    z = val + dt_ref[...]
    softplus = jnp.maximum(z, 0.0) + jnp.log1p(jnp.exp(-jnp.abs(z)))
    g = jnp.where((quantity >= 1) & (quantity < GATE_GROUPS), -jnp.exp(alog_ref[...]) * softplus, 0.0)
    r = lax.broadcasted_iota(jnp.int32, (tm, tm), 0)
    c = lax.broadcasted_iota(jnp.int32, (tm, tm), 1)
    tri = (((r // chunk) == (c // chunk)) & (c <= r)).astype(BF16)
    hi = g.astype(BF16)
    rem = g - hi.astype(F32)
    mid = rem.astype(BF16)
    lo = (rem - mid.astype(F32)).astype(BF16)
    gc = _dot(tri, hi) + _dot(tri, mid) + _dot(tri, lo)
    total = jnp.concatenate(
        [jnp.broadcast_to(gc[(ci + 1) * chunk - 1:(ci + 1) * chunk, :], (chunk, gc.shape[1]))
         for ci in range(tm // chunk)], axis=0)
    gt_ref[...] = jnp.where(quantity == 0, beta,
                  jnp.where(quantity == 1, gc,
                  jnp.where(quantity == 2, jnp.exp(gc),
                  jnp.where(quantity == 3, jnp.exp(total - gc),
                  jnp.where(quantity == 4, jnp.exp(total), 0.0)))))
    gct = gc.T
    for pi in range(tm // pw):
        for grp in range(gc.shape[1] // V7X_LANES):
            lo_row = grp * V7X_LANES + hg
            gr_ref[pi, grp * hg:(grp + 1) * hg, :] = gct[lo_row:lo_row + hg, pi * pw:(pi + 1) * pw]


def dn_gates(xn, w_gate, layer, a_log, dt_bias, *, hg):
    t, d = xn.shape
    n_heads = a_log.shape[0]
    gl = w_gate.shape[2]
    pw = DN_PACK * DN_CHUNK
    tm = _tile(t, 512)
    assert tm % pw == 0
    alog = _gate_lanes(jnp.zeros_like(a_log), a_log, hg).reshape(1, gl)
    dt = _gate_lanes(jnp.zeros_like(dt_bias), dt_bias, hg).reshape(1, gl)
    row = pl.BlockSpec((1, gl), lambda i: (0, 0))
    kern = functools.partial(_gates_kernel, hg=hg, chunk=DN_CHUNK)
    return pl.pallas_call(
        kern,
        grid=(t // tm,),
        in_specs=[pl.BlockSpec((tm, d), lambda i: (i, 0)),
                  pl.BlockSpec((None, d, gl), lambda i: (layer, 0, 0)), row, row],
        out_specs=[pl.BlockSpec((tm, gl), lambda i: (i, 0)),
                   pl.BlockSpec((tm // pw, n_heads, pw), lambda i: (i, 0, 0))],
        out_shape=[jax.ShapeDtypeStruct((t, gl), F32),
                   jax.ShapeDtypeStruct((t // pw, n_heads, pw), F32)],
        compiler_params=_params(("parallel",), tm * d * 2 + d * gl * 2 + 4 * tm * gl * 4 + 4 * tm * tm),
        name="dn_gates",
    )(xn, w_gate, alog, dt)


def _block_diag(xp):
    n, m = xp.shape
    tiled = jnp.concatenate([xp] * (m // n), axis=0)
    r = lax.broadcasted_iota(jnp.int32, (m, m), 0) // n
    c = lax.broadcasted_iota(jnp.int32, (m, m), 1) // n
    return jnp.where(r == c, tiled, jnp.zeros_like(tiled))


def _packed_dot(xp, yp, passes):
    xh = xp.astype(BF16)
    yh = yp.astype(BF16)
    out = _dot(xh, _block_diag(yh))
    if passes >= 3:
        xl = (xp - xh.astype(F32)).astype(BF16)
        yl = (yp - yh.astype(F32)).astype(BF16)
        out = out + _dot(xh, _block_diag(yl)) + _dot(xl, _block_diag(yh))
    return out


def _packed_unit_lower_inverses(aps, passes):
    n = aps[0].shape[0]
    r = lax.broadcasted_iota(jnp.int32, aps[0].shape, 0)
    c = lax.broadcasted_iota(jnp.int32, aps[0].shape, 1) % n
    eye = (r == c).astype(F32)
    size = INV_BASE
    in_blk = (r // size) == (c // size)
    xs = [jnp.where(in_blk, -ap, 0.0) for ap in aps]
    invs = [eye + x for x in xs]
    power = 1
    while 2 * power < size:
        xs = [_packed_dot(x, x, passes) for x in xs]
        invs = [inv + _packed_dot(inv, x, passes) for inv, x in zip(invs, xs)]
        power *= 2
    while size < n:
        merged = (r // (2 * size)) == (c // (2 * size))
        offs = [_packed_dot(jnp.where(merged & ~in_blk, ap, 0.0), inv, passes) for ap, inv in zip(aps, invs)]
        invs = [inv - _packed_dot(inv, off, passes) for inv, off in zip(invs, offs)]
        in_blk = merged
        size *= 2
    return invs


def _dn_kernel(q_ref, k_ref, v_ref, z_ref, wq_ref, wk_ref, wv_ref, gt_ref, gr_ref, nw_ref, o_ref,
               qtail, ktail, vtail, qn, kn, vn, rhs_s, u_s, wq_s, kd_s, qkm_s, vnew_s, qs_s, state,
               *, ts, hg, chunk):
    s = pl.program_id(2)
    gi = pl.program_id(1)
    hd = HEAD_DIM
    halo = V7X_SUBLANES

    @pl.when(s == 0)
    def _():
        state[...] = jnp.zeros(state.shape, F32)
        for tail in (qtail, ktail, vtail):
            tail[...] = jnp.zeros(tail.shape, F32)

    def conv_silu(src_ref, tail, w_ref):
        x = src_ref[...].astype(F32)
        xe = jnp.concatenate([tail[...], x], axis=0)
        tail[...] = x[ts - halo:, :]
        return _silu(_causal_conv(xe, w_ref[...])[halo:])

    def l2norm(x, scale):
        return x * (lax.rsqrt(jnp.sum(x * x, axis=-1, keepdims=True) + EPS) * scale)

    qc = conv_silu(q_ref, qtail, wq_ref)
    kc = conv_silu(k_ref, ktail, wk_ref)
    vn[...] = conv_silu(v_ref, vtail, wv_ref)
    for hh in range(hg):
        cols = slice(hh * hd, (hh + 1) * hd)
        qn[:, cols] = l2norm(qc[:, cols], hd ** -0.5)
        kn[:, cols] = l2norm(kc[:, cols], 1.0)

    pack = DN_PACK
    pw = pack * chunk
    n_pack = ts // pw
    r = lax.broadcasted_iota(jnp.int32, (pw, pw), 0)
    c = lax.broadcasted_iota(jnp.int32, (pw, pw), 1)
    same = (r // chunk) == (c // chunk)
    tri = same & (r >= c)
    strict = same & (r > c)
    nw = nw_ref[...]

    jobs = [(hh, p) for hh in range(hg) for p in range(n_pack)]
    chunk_decay = {}
    aps = []
    for hh, p in jobs:
        cols = slice(hh * hd, (hh + 1) * hd)
        rows = slice(p * pw, (p + 1) * pw)
        beta, gc, eg, erem, etot = (gt_ref[rows, qi * hg + hh:qi * hg + hh + 1] for qi in range(GATE_GROUPS))
        gc_row = gr_ref[p, pl.ds(gi * hg + hh, 1), :]
        q = qn[rows, cols]
        k = kn[rows, cols]
        v = vn[rows, cols]
        kb = k.astype(BF16)
        sims = _dot_nt(jnp.concatenate([kb, q.astype(BF16)], axis=0), kb)
        decay = jnp.exp(jnp.where(tri, gc - gc_row, -jnp.inf))
        a_full = jnp.where(strict, beta * sims[0:pw] * decay, 0.0)
        ap = a_full[0:chunk]
        for bb in range(1, pack):
            ap = ap + a_full[bb * chunk:(bb + 1) * chunk]
        aps.append(ap)
        rhs_s[hh, rows, :] = jnp.concatenate([v * beta, k * (beta * eg)], axis=1).astype(BF16)
        qg = (q * eg).astype(BF16)
        for bb in range(pack):
            wq_s[hh, p * pack + bb, chunk:2 * chunk, :] = qg[bb * chunk:(bb + 1) * chunk]
            chunk_decay[(hh, p * pack + bb)] = etot[bb * chunk:bb * chunk + 1, :]
        kd_s[hh, rows, :] = (k * erem).astype(BF16)
        qkm_s[hh, p] = (sims[pw:2 * pw] * decay).astype(BF16)

    tps = _packed_unit_lower_inverses(aps, INV_PASSES)

    for (hh, p), tp in zip(jobs, tps):
        rows = slice(p * pw, (p + 1) * pw)
        uw = _dot(_block_diag(tp.astype(BF16)), rhs_s[hh, rows, :])
        u_s[hh, rows, :] = uw[:, 0:hd]
        w = uw[:, hd:2 * hd].astype(BF16)
        for bb in range(pack):
            wq_s[hh, p * pack + bb, 0:chunk, :] = w[bb * chunk:(bb + 1) * chunk]

    for cidx in range(ts // chunk):
        rows = slice(cidx * chunk, (cidx + 1) * chunk)
        for hh in range(hg):
            st = state[hh]
            ws_qs = _dot(wq_s[hh, cidx], st.astype(BF16))
            v_new = (u_s[hh, rows, :] - ws_qs[0:chunk]).astype(BF16)
            vnew_s[hh, rows, :] = v_new
            qs_s[hh, rows, :] = ws_qs[chunk:2 * chunk]
            state[hh] = st * chunk_decay[(hh, cidx)] + _dot_tn(kd_s[hh, rows, :], v_new)

    for hh in range(hg):
        cols = slice(hh * hd, (hh + 1) * hd)
        for p in range(n_pack):
            rows = slice(p * pw, (p + 1) * pw)
            o = qs_s[hh, rows, :] + _dot(qkm_s[hh, p], vnew_s[hh, rows, :])
            zz = z_ref[rows, cols].astype(F32)
            o = o * lax.rsqrt(jnp.mean(o * o, axis=-1, keepdims=True) + EPS) * nw * _silu(zz)
            o_ref[rows, cols] = o.astype(o_ref.dtype)


def delta_net(proj, proj_z, gt, gr, conv_w, layer, norm_w, *, seq, q_off, k_off, v_off, z_off, n_heads):
    b = proj.shape[0]
    hd = HEAD_DIM
    hg = min(DN_HEADS_PER_STEP, n_heads)
    pw = DN_PACK * DN_CHUNK
    ts = _tile(seq, 512, align=pw)
    gw = hg * hd
    kw = n_heads * hd
    assert n_heads % hg == 0
    for off in (q_off, k_off, v_off, z_off, kw):
        assert off % gw == 0
    kern = functools.partial(_dn_kernel, ts=ts, hg=hg, chunk=DN_CHUNK)

    def col_spec(off):
        return pl.BlockSpec((None, ts, gw), lambda bi, g, s: (bi, s, off // gw + g))

    def w_spec(off):
        return pl.BlockSpec((None, DN_CONV, gw), lambda bi, g, s: (layer, 0, off // gw + g))

    scratch = ([pltpu.VMEM((V7X_SUBLANES, gw), F32)] * 3
               + [pltpu.VMEM((ts, gw), F32)] * 3
               + [pltpu.VMEM((hg, ts, 2 * hd), BF16),
                  pltpu.VMEM((hg, ts, hd), F32),
                  pltpu.VMEM((hg, ts // DN_CHUNK, 2 * DN_CHUNK, hd), BF16),
                  pltpu.VMEM((hg, ts, hd), BF16),
                  pltpu.VMEM((hg, ts // pw, pw, pw), BF16),
                  pltpu.VMEM((hg, ts, hd), BF16),
                  pltpu.VMEM((hg, ts, hd), F32),
                  pltpu.VMEM((hg, hd, hd), F32)])
    block_bytes = (5 * ts * gw * 2 + ts * V7X_LANES * 4 + (ts // pw) * n_heads * pw * 4
                   + 3 * (ts + V7X_SUBLANES) * gw * 4 + 3 * ts * gw * 4 + hg * ts * hd * 18)
    return pl.pallas_call(
        kern,
        grid=(b, n_heads // hg, seq // ts),
        in_specs=[col_spec(q_off), col_spec(k_off), col_spec(v_off), col_spec(z_off),
                  w_spec(0), w_spec(kw), w_spec(2 * kw),
                  pl.BlockSpec((None, ts, V7X_LANES), lambda bi, g, s: (bi, s, g)),
                  pl.BlockSpec((None, ts // pw, n_heads, pw), lambda bi, g, s: (bi, s, 0, 0)),
                  pl.BlockSpec((1, hd), lambda bi, g, s: (0, 0))],
        out_specs=pl.BlockSpec((None, ts, gw), lambda bi, g, s: (bi, s, g)),
        out_shape=jax.ShapeDtypeStruct((b, seq, kw), BF16),
        scratch_shapes=scratch,
        compiler_params=_params(("parallel", "parallel", "arbitrary"), block_bytes),
        name="delta_net",
    )(proj, proj, proj, proj_z, conv_w, conv_w, conv_w, gt, gr, norm_w.reshape(1, hd))


def _merge_kernel(oa_ref, od_ref, ga_ref, gd_ref, wa_ref, wd_ref, y_ref):
    a = _dot(oa_ref[...], wa_ref[...])
    sa = jax.nn.sigmoid(ga_ref[...].astype(F32))
    sd = jax.nn.sigmoid(gd_ref[...].astype(F32))
    d = _dot(od_ref[...], wd_ref[...])
    y_ref[...] = (sa * a + sd * d).astype(y_ref.dtype)


def branch_merge(o_a, o_d, proj, w_a, w_d, layer, *, ga_off, gd_off):
    t, ka = o_a.shape
    kd = o_d.shape[1]
    n = w_a.shape[2]
    tm = _tile(t, 1024)
    tn = _tile(n, 512)
    assert ga_off % tn == 0 and gd_off % tn == 0
    block_bytes = 2 * (tm * ka + tm * kd + 3 * tm * tn + ka * tn + kd * tn)
    return pl.pallas_call(
        _merge_kernel,
        grid=(t // tm, n // tn),
        in_specs=[pl.BlockSpec((tm, ka), lambda i, j: (i, 0)),
                  pl.BlockSpec((tm, kd), lambda i, j: (i, 0)),
                  pl.BlockSpec((tm, tn), lambda i, j: (i, ga_off // tn + j)),
                  pl.BlockSpec((tm, tn), lambda i, j: (i, gd_off // tn + j)),
                  pl.BlockSpec((None, ka, tn), lambda i, j: (layer, 0, j)),
                  pl.BlockSpec((None, kd, tn), lambda i, j: (layer, 0, j))],
        out_specs=pl.BlockSpec((tm, tn), lambda i, j: (i, j)),
        out_shape=jax.ShapeDtypeStruct((t, n), BF16),
        compiler_params=_params(("parallel", "parallel"), block_bytes),
        name="branch_merge",
    )(o_a, o_d, proj, proj, w_a, w_d)


def _proj_norm_res_kernel(*refs, with_next):
    if with_next:
        x_ref, w_ref, h_ref, post_ref, next_ref, h_out, xn_out = refs
    else:
        x_ref, w_ref, h_ref, post_ref, h_out = refs
    hn = h_ref[...] + _rms(_dot(x_ref[...], w_ref[...]), post_ref[...])
    h_out[...] = hn
    if with_next:
        xn_out[...] = _rms(hn, next_ref[...]).astype(xn_out.dtype)


def proj_norm_residual(x, w, layer, h, post_w, next_w):
    t, k = x.shape
    d = w.shape[2]
    tm = _tile(t, 256)
    with_next = next_w is not None
    kern = functools.partial(_proj_norm_res_kernel, with_next=with_next)
    row = pl.BlockSpec((1, d), lambda i: (0, 0))
    tile = pl.BlockSpec((tm, d), lambda i: (i, 0))
    in_specs = [pl.BlockSpec((tm, k), lambda i: (i, 0)),
                pl.BlockSpec((None, k, d), lambda i: (layer, 0, 0), pipeline_mode=pl.Buffered(1)), tile, row]
    args = [x, w, h, post_w.reshape(1, d)]
    out_specs = [tile]
    out_shape = [jax.ShapeDtypeStruct((t, d), F32)]
    if with_next:
        in_specs.append(row)
        args.append(next_w.reshape(1, d))
        out_specs.append(tile)
        out_shape.append(jax.ShapeDtypeStruct((t, d), BF16))
    block_bytes = tm * k * 2 + tm * d * (4 + 4 + 2) + (k * d * 2) // 2 + tm * d * 4
    res = pl.pallas_call(
        kern,
        grid=(t // tm,),
        in_specs=in_specs,
        out_specs=out_specs,
        out_shape=out_shape,
        compiler_params=pltpu.CompilerParams(
            dimension_semantics=("parallel",),
            vmem_limit_bytes=int(min(V7X_VMEM_BYTES - (2 << 20), 2 * block_bytes + (4 << 20)))),
        name="proj_norm_residual",
    )(*args)
    return (res[0], res[1]) if with_next else (res[0], None)


def _causal_conv(u, taps):
    y = u * taps[0:1, :]
    for j in range(1, taps.shape[0]):
        y = pltpu.roll(y, 1, 0) + u * taps[j:j + 1, :]
    return y


def _ffn_up_kernel(x_ref, wg_ref, wv_ref, cg_ref, cv_ref, bg_ref, bv_ref, o_ref, tail_g, tail_v,
                   *, tiles_per_seq, sub):
    i = pl.program_id(1)
    halo = V7X_SUBLANES
    tm, tn = o_ref.shape
    x = x_ref[...]
    blocks = [slice(lo, min(lo + sub, tn)) for lo in range(0, tn, sub)]

    @pl.when(i == 0)
    def _():
        tail_g[...] = jnp.zeros(tail_g.shape, F32)
        tail_v[...] = jnp.zeros(tail_v.shape, F32)

    def project(cols):
        return _dot(x, wg_ref[:, cols]), _dot(x, wv_ref[:, cols])

    def conv(u, tail, cols, c_ref, b_ref):
        hist = jnp.where((i % tiles_per_seq) != 0, tail[:, cols], 0.0)
        tail[:, cols] = u[tm - halo:, :]
        return _causal_conv(jnp.concatenate([hist, u], axis=0), c_ref[:, cols])[halo:] + b_ref[:, cols]

    pending = project(blocks[0])
    for s, cols in enumerate(blocks):
        ug, uv = pending
        if s + 1 < len(blocks):
            pending = project(blocks[s + 1])
        gate = conv(ug, tail_g, cols, cg_ref, bg_ref)
        val = conv(uv, tail_v, cols, cv_ref, bv_ref)
        o_ref[:, cols] = (_silu(gate) * val).astype(o_ref.dtype)


def ffn_up(xn, w_up, conv_w, conv_b, layer, *, seq):
    t, d = xn.shape
    dff = w_up.shape[2] // 2
    tm = _tile(seq, 1024)
    sub = 2 * V7X_LANES
    tn = _tile(dff, 2816)
    halo = V7X_SUBLANES
    nj = dff // tn
    kern = functools.partial(_ffn_up_kernel, tiles_per_seq=seq // tm, sub=sub)

    def w_spec(off):
        return pl.BlockSpec((None, d, tn), lambda j, i: (layer, 0, off + j), pipeline_mode=pl.Buffered(1))

    block_bytes = tm * d * 2 + tm * tn * 2 + d * tn * 2 + 2 * (tm + halo) * sub * 4
    return pl.pallas_call(
        kern,
        grid=(nj, t // tm),
        in_specs=[pl.BlockSpec((tm, d), lambda j, i: (i, 0)),
                  w_spec(0), w_spec(nj),
                  pl.BlockSpec((None, FFN_CONV, tn), lambda j, i: (layer, 0, j)),
                  pl.BlockSpec((None, FFN_CONV, tn), lambda j, i: (layer, 0, nj + j)),
                  pl.BlockSpec((None, 1, tn), lambda j, i: (layer, 0, j)),
                  pl.BlockSpec((None, 1, tn), lambda j, i: (layer, 0, nj + j))],
        out_specs=pl.BlockSpec((tm, tn), lambda j, i: (i, j)),
        out_shape=jax.ShapeDtypeStruct((t, dff), BF16),
        scratch_shapes=[pltpu.VMEM((halo, tn), F32)] * 2,
        compiler_params=_params(("arbitrary", "arbitrary"), block_bytes),
        name="ffn_up",
    )(xn, w_up, w_up, conv_w, conv_w, conv_b, conv_b)


def kernel(x, positions, norm_mix_pre, w_in, attn_sinks, dn_conv_w, dn_a_log, dn_dt_bias, dn_norm_w,
           w_branch_attn, w_branch_dn, w_out, norm_mix_post, norm_ffn_pre, w_ffn_up, ffn_conv_w,
           ffn_conv_b, w_ffn_down, norm_ffn_post):
    b, seq, d = x.shape
    depth = w_in.shape[0]
    t = b * seq
    n_attn = attn_sinks.shape[1]
    n_dn = dn_a_log.shape[1]
    qa_w = n_attn * HEAD_DIM
    kv_w = ATTN_KV_HEADS * HEAD_DIM
    dn_w = n_dn * HEAD_DIM

    edges = [0]
    for wdt in (qa_w, kv_w, kv_w, 3 * dn_w, n_dn, n_dn, dn_w, d, d):
        edges.append(edges[-1] + wdt)
    w_in_b = w_in.astype(BF16)
    head_w = edges[4]
    w_tail = w_in_b[:, :, edges[6]:edges[9]]
    tail_w = edges[9] - edges[6]
    dn_hg = min(DN_HEADS_PER_STEP, n_dn)
    w_gate = _gate_lanes(w_in_b[:, :, edges[4]:edges[5]], w_in_b[:, :, edges[5]:edges[6]], dn_hg)
    q_off, k_off, v_off = 0, qa_w, qa_w + kv_w
    dq_off = qa_w + 2 * kv_w
    dk_off, dv_off = dq_off + dn_w, dq_off + 2 * dn_w
    z_off, ga_off, gd_off = 0, dn_w, dn_w + d

    w_ba = w_branch_attn.astype(BF16)
    w_bd = w_branch_dn.astype(BF16)
    w_o = w_out.astype(BF16)
    w_up = w_ffn_up.astype(BF16)
    w_dn = w_ffn_down.astype(BF16)
    ffn_b = ffn_conv_b.reshape(depth, 1, ffn_conv_b.shape[1])

    cos, sin = rope_tables(positions)
    cos = cos.reshape(b, seq, HEAD_DIM)
    sin = sin.reshape(b, seq, HEAD_DIM)

    h = x.reshape(t, d)
    xn = rmsnorm(h, norm_mix_pre[0])
    for l in range(depth):
        proj_head = matmul(xn, w_in_b, l, head_w)
        proj_tail = matmul(xn, w_tail, l, tail_w)
        head3 = proj_head.reshape(b, seq, head_w)
        o_a = attention(head3, cos, sin, attn_sinks[l], seq=seq, q_off=q_off, k_off=k_off,
                        v_off=v_off, n_heads=n_attn)
        gt, gr = dn_gates(xn, w_gate, l, dn_a_log[l], dn_dt_bias[l], hg=dn_hg)
        o_d = delta_net(head3, proj_tail.reshape(b, seq, tail_w), gt.reshape(b, seq, gt.shape[1]),
                        gr.reshape(b, seq // (DN_PACK * DN_CHUNK), n_dn, DN_PACK * DN_CHUNK),
                        dn_conv_w, l, dn_norm_w[l],
                        seq=seq, q_off=dq_off, k_off=dk_off, v_off=dv_off, z_off=z_off, n_heads=n_dn)
        y = branch_merge(o_a.reshape(t, qa_w), o_d.reshape(t, dn_w), proj_tail, w_ba, w_bd, l,
                         ga_off=ga_off, gd_off=gd_off)
        h, xf = proj_norm_residual(y, w_o, l, h, norm_mix_post[l], norm_ffn_pre[l])
        u = ffn_up(xf, w_up, ffn_conv_w, ffn_b, l, seq=seq)
        next_w = norm_mix_pre[l + 1] if l + 1 < depth else None
        h, xn = proj_norm_residual(u, w_dn, l, h, norm_ffn_post[l], next_w)
    return h.reshape(b, seq, d)
```

```python
import functools

import jax
import jax.numpy as jnp
from jax import lax
from jax.experimental import pallas as pl
from jax.experimental.pallas import tpu as pltpu

HEAD_DIM = 128
ATTN_KV_HEADS = 4
ATTN_BLOCK = 128
ROT_DIM = HEAD_DIM // 4
ROPE_THETA = 500000.0
DN_CHUNK = 64
DN_CONV = 4
FFN_CONV = 3
EPS = 1e-6
LOG2E = 1.4426950408889634

V7X_LANES = 128
V7X_SUBLANES = 8
V7X_VMEM_BYTES = 64 * 1024 * 1024

DN_PACK = 2 * V7X_LANES // DN_CHUNK
DN_HEADS_PER_STEP = 8
INV_PASSES = 1
INV_BASE = 16

F32 = jnp.float32
BF16 = jnp.bfloat16


def _dot(a, b):
    return jnp.dot(a, b, preferred_element_type=F32)


def _dot_nt(a, b):
    return lax.dot_general(a, b, (((1,), (1,)), ((), ())), preferred_element_type=F32)


def _dot_tn(a, b):
    return lax.dot_general(a, b, (((0,), (0,)), ((), ())), preferred_element_type=F32)


def _rms(x, w):
    return x * lax.rsqrt(jnp.mean(x * x, axis=-1, keepdims=True) + EPS) * w


def _silu(x):
    h = 0.5 * x
    return h + h * jnp.tanh(h)


def _vmem_limit(block_bytes):
    return int(min(V7X_VMEM_BYTES - (2 << 20), 2 * block_bytes + (12 << 20)))


def _params(semantics, block_bytes):
    return pltpu.CompilerParams(dimension_semantics=semantics,
                                vmem_limit_bytes=_vmem_limit(block_bytes))


def _tile(n, want, align=V7X_LANES):
    for t in range(min(n, want) // align * align, 0, -align):
        if n % t == 0:
            return t
    raise ValueError(f"no {align}-aligned tile divides {n}")


def _rmsnorm_kernel(x_ref, w_ref, o_ref):
    o_ref[...] = _rms(x_ref[...], w_ref[...]).astype(o_ref.dtype)


def rmsnorm(x, w):
    t, d = x.shape
    tm = _tile(t, 512)
    return pl.pallas_call(
        _rmsnorm_kernel,
        grid=(t // tm,),
        in_specs=[pl.BlockSpec((tm, d), lambda i: (i, 0)),
                  pl.BlockSpec((1, d), lambda i: (0, 0))],
        out_specs=pl.BlockSpec((tm, d), lambda i: (i, 0)),
        out_shape=jax.ShapeDtypeStruct((t, d), BF16),
        compiler_params=_params(("parallel",), tm * d * 6),
        name="rmsnorm",
    )(x, w.reshape(1, d))


def _matmul_kernel(x_ref, w_ref, o_ref):
    o_ref[...] = _dot(x_ref[...], w_ref[...]).astype(o_ref.dtype)


def matmul(x, w, layer, n):
    t, k = x.shape
    tm = _tile(t, 1024)
    tn = _tile(n, 1024)
    return pl.pallas_call(
        _matmul_kernel,
        grid=(t // tm, n // tn),
        in_specs=[pl.BlockSpec((tm, k), lambda i, j: (i, 0)),
                  pl.BlockSpec((None, k, tn), lambda i, j: (layer, 0, j))],
        out_specs=pl.BlockSpec((tm, tn), lambda i, j: (i, j)),
        out_shape=jax.ShapeDtypeStruct((t, n), BF16),
        compiler_params=_params(("parallel", "parallel"), 2 * (tm * k + k * tn + tm * tn)),
        name="in_proj",
    )(x, w)


def _rope_table_kernel(pos_ref, freq_ref, sign_ref, cos_ref, sin_ref):
    ang = pos_ref[...] * freq_ref[...]
    cos_ref[...] = jnp.cos(ang)
    sin_ref[...] = jnp.sin(ang) * sign_ref[...]


def rope_tables(positions):
    t = positions.size
    half = ROT_DIM // 2
    inv_freq = ROPE_THETA ** (-jnp.arange(0, ROT_DIM, 2, dtype=F32) / ROT_DIM)
    freq = jnp.concatenate([inv_freq, inv_freq, jnp.zeros((HEAD_DIM - ROT_DIM,), F32)]).reshape(1, HEAD_DIM)
    sign = jnp.concatenate([-jnp.ones((half,), F32), jnp.ones((HEAD_DIM - half,), F32)]).reshape(1, HEAD_DIM)
    pos = positions.astype(F32).reshape(t, 1)
    tm = _tile(t, 1024)
    row = pl.BlockSpec((1, HEAD_DIM), lambda i: (0, 0))
    out = pl.BlockSpec((tm, HEAD_DIM), lambda i: (i, 0))
    return pl.pallas_call(
        _rope_table_kernel,
        grid=(t // tm,),
        in_specs=[pl.BlockSpec((tm, 1), lambda i: (i, 0)), row, row],
        out_specs=[out, out],
        out_shape=[jax.ShapeDtypeStruct((t, HEAD_DIM), F32)] * 2,
        compiler_params=_params(("parallel",), 3 * tm * HEAD_DIM * 4),
        name="rope_tables",
    )(pos, freq, sign)


def _rotate_half_matrix():
    half = ROT_DIM // 2
    r = jnp.arange(HEAD_DIM)[:, None]
    c = jnp.arange(HEAD_DIM)[None, :]
    p = ((c < half) & (r == c + half)) | ((c >= half) & (c < ROT_DIM) & (r == c - half))
    return p.astype(BF16)


def _attn_kernel(sink_ref, q_ref, k_ref, v_ref, cos_ref, sin_ref, perm_ref, o_ref, kbuf, vbuf,
                 *, tq, n_kv, group):
    i = pl.program_id(1)
    blk = ATTN_BLOCK
    hd = HEAD_DIM

    @pl.when(i == 0)
    def _():
        kbuf[0:blk, :] = jnp.zeros((blk, n_kv * hd), kbuf.dtype)
        vbuf[0:blk, :] = jnp.zeros((blk, n_kv * hd), vbuf.dtype)

    @pl.when(i > 0)
    def _():
        kbuf[0:blk, :] = kbuf[tq:tq + blk, :]
        vbuf[0:blk, :] = vbuf[tq:tq + blk, :]

    perm = perm_ref[...]

    def rope(xb, c, s):
        return (xb.astype(F32) * c + _dot(xb, perm) * s).astype(BF16)

    cos = cos_ref[...]
    sin = sin_ref[...]
    for kh in range(n_kv):
        cols = slice(kh * hd, (kh + 1) * hd)
        kbuf[blk:blk + tq, cols] = rope(k_ref[:, cols], cos, sin)
    vbuf[blk:blk + tq, :] = v_ref[...]

    c = lax.broadcasted_iota(jnp.int32, (2 * blk, group * blk), 0)
    r = lax.broadcasted_iota(jnp.int32, (2 * blk, group * blk), 1) % blk
    band = (c > r) & (c <= r + blk)
    band_first = band & (c >= blk)
    scale = hd ** -0.5 * LOG2E
    kv_heads = range(n_kv)

    def head_ids(kh):
        return [kh * group + g for g in range(group)]

    for j in range(tq // blk):
        rows = slice(j * blk, (j + 1) * blk)
        keys = slice(j * blk, (j + 2) * blk)
        cj = cos[rows]
        sj = sin[rows]
        mask = band if j > 0 else (band_first | (band & (i > 0)))
        qs = [jnp.concatenate([rope(q_ref[rows, h * hd:(h + 1) * hd], cj, sj) for h in head_ids(kh)], axis=0)
              for kh in kv_heads]
        logits = [jnp.where(mask, _dot_nt(kbuf[keys, kh * hd:(kh + 1) * hd], qs[kh]) * scale, -jnp.inf)
                  for kh in kv_heads]
        sinks = [jnp.concatenate([jnp.full((1, blk), sink_ref[h] * LOG2E, F32) for h in head_ids(kh)], axis=1)
                 for kh in kv_heads]
        ms = [jnp.maximum(jnp.max(logits[kh], axis=0, keepdims=True), sinks[kh]) for kh in kv_heads]
        ps = [jnp.exp2(logits[kh] - ms[kh]) for kh in kv_heads]
        denoms = [jnp.sum(ps[kh], axis=0, keepdims=True) + jnp.exp2(sinks[kh] - ms[kh]) for kh in kv_heads]
        outs = [(_dot_tn(vbuf[keys, kh * hd:(kh + 1) * hd], ps[kh].astype(BF16)) / denoms[kh]).T
                for kh in kv_heads]
        for kh in kv_heads:
            for g, h in enumerate(head_ids(kh)):
                o_ref[rows, h * hd:(h + 1) * hd] = outs[kh][g * blk:(g + 1) * blk].astype(o_ref.dtype)


def attention(proj, cos, sin, sinks, *, seq, q_off, k_off, v_off, n_heads):
    b = proj.shape[0]
    n_kv = ATTN_KV_HEADS
    group = n_heads // n_kv
    qw = n_heads * HEAD_DIM
    kw = n_kv * HEAD_DIM
    tq = _tile(seq, 512)
    assert q_off % qw == 0 and k_off % kw == 0 and v_off % kw == 0
    kern = functools.partial(_attn_kernel, tq=tq, n_kv=n_kv, group=group)
    block_bytes = tq * (2 * qw + 2 * kw) * 2 + 2 * tq * HEAD_DIM * 4 + 2 * (tq + ATTN_BLOCK) * kw * 2
    return pl.pallas_call(
        kern,
        grid=(b, seq // tq),
        in_specs=[pl.BlockSpec(memory_space=pltpu.SMEM),
                  pl.BlockSpec((None, tq, qw), lambda bi, i: (bi, i, q_off // qw)),
                  pl.BlockSpec((None, tq, kw), lambda bi, i: (bi, i, k_off // kw)),
                  pl.BlockSpec((None, tq, kw), lambda bi, i: (bi, i, v_off // kw)),
                  pl.BlockSpec((None, tq, HEAD_DIM), lambda bi, i: (bi, i, 0)),
                  pl.BlockSpec((None, tq, HEAD_DIM), lambda bi, i: (bi, i, 0)),
                  pl.BlockSpec((HEAD_DIM, HEAD_DIM), lambda bi, i: (0, 0))],
        out_specs=pl.BlockSpec((None, tq, qw), lambda bi, i: (bi, i, 0)),
        out_shape=jax.ShapeDtypeStruct((b, seq, qw), BF16),
        scratch_shapes=[pltpu.VMEM((tq + ATTN_BLOCK, kw), BF16),
                        pltpu.VMEM((tq + ATTN_BLOCK, kw), BF16)],
        compiler_params=_params(("parallel", "arbitrary"), block_bytes),
        name="swa_attention",
    )(sinks, proj, proj, proj, cos, sin, _rotate_half_matrix())


GATE_GROUPS = 5


def _gate_lanes(b_part, a_part, hg):
    n_heads = a_part.shape[-1]
    assert n_heads % hg == 0 and GATE_GROUPS * hg <= V7X_LANES
    pad = jnp.zeros(a_part.shape[:-1] + (V7X_LANES - GATE_GROUPS * hg,), a_part.dtype)
    parts = []
    for g in range(n_heads // hg):
        heads = slice(g * hg, (g + 1) * hg)
        parts += [b_part[..., heads]] + [a_part[..., heads]] * (GATE_GROUPS - 1) + [pad]
    return jnp.concatenate(parts, axis=-1)


def _gates_kernel(x_ref, w_ref, alog_ref, dt_ref, gt_ref, gr_ref, *, hg, chunk):
    tm = x_ref.shape[0]
    pw = DN_PACK * chunk
    val = _dot(x_ref[...], w_ref[...])
    quantity = (lax.broadcasted_iota(jnp.int32, val.shape, 1) % V7X_LANES) // hg
    beta = jax.nn.sigmoid(val)
    z = val + dt_ref[...]
    softplus = jnp.maximum(z, 0.0) + jnp.log1p(jnp.exp(-jnp.abs(z)))
    g = jnp.where((quantity >= 1) & (quantity < GATE_GROUPS), -jnp.exp(alog_ref[...]) * softplus, 0.0)
    r = lax.broadcasted_iota(jnp.int32, (tm, tm), 0)
    c = lax.broadcasted_iota(jnp.int32, (tm, tm), 1)
    tri = (((r // chunk) == (c // chunk)) & (c <= r)).astype(BF16)
    hi = g.astype(BF16)
    rem = g - hi.astype(F32)
    mid = rem.astype(BF16)
    lo = (rem - mid.astype(F32)).astype(BF16)
    gc = _dot(tri, hi) + _dot(tri, mid) + _dot(tri, lo)
    total = jnp.concatenate(
        [jnp.broadcast_to(gc[(ci + 1) * chunk - 1:(ci + 1) * chunk, :], (chunk, gc.shape[1]))
         for ci in range(tm // chunk)], axis=0)
    gt_ref[...] = jnp.where(quantity == 0, beta,
                  jnp.where(quantity == 1, gc,
                  jnp.where(quantity == 2, jnp.exp(gc),
                  jnp.where(quantity == 3, jnp.exp(total - gc),
                  jnp.where(quantity == 4, jnp.exp(total), 0.0)))))
    gct = gc.T
    for pi in range(tm // pw):
        for grp in range(gc.shape[1] // V7X_LANES):
            lo_row = grp * V7X_LANES + hg
            gr_ref[pi, grp * hg:(grp + 1) * hg, :] = gct[lo_row:lo_row + hg, pi * pw:(pi + 1) * pw]


def dn_gates(xn, w_gate, layer, a_log, dt_bias, *, hg):
    t, d = xn.shape
    n_heads = a_log.shape[0]
    gl = w_gate.shape[2]
    pw = DN_PACK * DN_CHUNK
    tm = _tile(t, 512)
    assert tm % pw == 0
    alog = _gate_lanes(jnp.zeros_like(a_log), a_log, hg).reshape(1, gl)
    dt = _gate_lanes(jnp.zeros_like(dt_bias), dt_bias, hg).reshape(1, gl)
    row = pl.BlockSpec((1, gl), lambda i: (0, 0))
    kern = functools.partial(_gates_kernel, hg=hg, chunk=DN_CHUNK)
    return pl.pallas_call(
        kern,
        grid=(t // tm,),
        in_specs=[pl.BlockSpec((tm, d), lambda i: (i, 0)),
                  pl.BlockSpec((None, d, gl), lambda i: (layer, 0, 0)), row, row],
        out_specs=[pl.BlockSpec((tm, gl), lambda i: (i, 0)),
                   pl.BlockSpec((tm // pw, n_heads, pw), lambda i: (i, 0, 0))],
        out_shape=[jax.ShapeDtypeStruct((t, gl), F32),
                   jax.ShapeDtypeStruct((t // pw, n_heads, pw), F32)],
        compiler_params=_params(("parallel",), tm * d * 2 + d * gl * 2 + 4 * tm * gl * 4 + 4 * tm * tm),
        name="dn_gates",
    )(xn, w_gate, alog, dt)


def _block_diag(xp):
    n, m = xp.shape
    tiled = jnp.concatenate([xp] * (m // n), axis=0)
    r = lax.broadcasted_iota(jnp.int32, (m, m), 0) // n
    c = lax.broadcasted_iota(jnp.int32, (m, m), 1) // n
    return jnp.where(r == c, tiled, jnp.zeros_like(tiled))


def _packed_dot(xp, yp, passes):
    xh = xp.astype(BF16)
    yh = yp.astype(BF16)
    out = _dot(xh, _block_diag(yh))
    if passes >= 3:
        xl = (xp - xh.astype(F32)).astype(BF16)
        yl = (yp - yh.astype(F32)).astype(BF16)
        out = out + _dot(xh, _block_diag(yl)) + _dot(xl, _block_diag(yh))
    return out


def _packed_unit_lower_inverses(aps, passes):
    n = aps[0].shape[0]
    r = lax.broadcasted_iota(jnp.int32, aps[0].shape, 0)
    c = lax.broadcasted_iota(jnp.int32, aps[0].shape, 1) % n
    eye = (r == c).astype(F32)
    size = INV_BASE
    in_blk = (r // size) == (c // size)
    xs = [jnp.where(in_blk, -ap, 0.0) for ap in aps]
    invs = [eye + x for x in xs]
    power = 1
    while 2 * power < size:
        xs = [_packed_dot(x, x, passes) for x in xs]
        invs = [inv + _packed_dot(inv, x, passes) for inv, x in zip(invs, xs)]
        power *= 2
    while size < n:
        merged = (r // (2 * size)) == (c // (2 * size))
        offs = [_packed_dot(jnp.where(merged & ~in_blk, ap, 0.0), inv, passes) for ap, inv in zip(aps, invs)]
        invs = [inv - _packed_dot(inv, off, passes) for inv, off in zip(invs, offs)]
        in_blk = merged
        size *= 2
    return invs


def _dn_kernel(q_ref, k_ref, v_ref, z_ref, wq_ref, wk_ref, wv_ref, gt_ref, gr_ref, nw_ref, o_ref,
               qtail, ktail, vtail, qn, kn, vn, rhs_s, u_s, wq_s, kd_s, qkm_s, vnew_s, qs_s, state,
               *, ts, hg, chunk):
    s = pl.program_id(2)
    gi = pl.program_id(1)
    hd = HEAD_DIM
    halo = V7X_SUBLANES

    @pl.when(s == 0)
    def _():
        state[...] = jnp.zeros(state.shape, F32)
        for tail in (qtail, ktail, vtail):
            tail[...] = jnp.zeros(tail.shape, F32)

    def conv_silu(src_ref, tail, w_ref):
        x = src_ref[...].astype(F32)
        xe = jnp.concatenate([tail[...], x], axis=0)
        tail[...] = x[ts - halo:, :]
        return _silu(_causal_conv(xe, w_ref[...])[halo:])

    def l2norm(x, scale):
        return x * (lax.rsqrt(jnp.sum(x * x, axis=-1, keepdims=True) + EPS) * scale)

    qc = conv_silu(q_ref, qtail, wq_ref)
    kc = conv_silu(k_ref, ktail, wk_ref)
    vn[...] = conv_silu(v_ref, vtail, wv_ref)
    for hh in range(hg):
        cols = slice(hh * hd, (hh + 1) * hd)
        qn[:, cols] = l2norm(qc[:, cols], hd ** -0.5)
        kn[:, cols] = l2norm(kc[:, cols], 1.0)

    pack = DN_PACK
    pw = pack * chunk
    n_pack = ts // pw
    r = lax.broadcasted_iota(jnp.int32, (pw, pw), 0)
    c = lax.broadcasted_iota(jnp.int32, (pw, pw), 1)
    same = (r // chunk) == (c // chunk)
    tri = same & (r >= c)
    strict = same & (r > c)
    nw = nw_ref[...]

    jobs = [(hh, p) for hh in range(hg) for p in range(n_pack)]
    chunk_decay = {}
    aps = []
    for hh, p in jobs:
        cols = slice(hh * hd, (hh + 1) * hd)
        rows = slice(p * pw, (p + 1) * pw)
        beta, gc, eg, erem, etot = (gt_ref[rows, qi * hg + hh:qi * hg + hh + 1] for qi in range(GATE_GROUPS))
        gc_row = gr_ref[p, pl.ds(gi * hg + hh, 1), :]
        q = qn[rows, cols]
        k = kn[rows, cols]
        v = vn[rows, cols]
        kb = k.astype(BF16)
        sims = _dot_nt(jnp.concatenate([kb, q.astype(BF16)], axis=0), kb)
        decay = jnp.exp(jnp.where(tri, gc - gc_row, -jnp.inf))
        a_full = jnp.where(strict, beta * sims[0:pw] * decay, 0.0)
        ap = a_full[0:chunk]
        for bb in range(1, pack):
            ap = ap + a_full[bb * chunk:(bb + 1) * chunk]
        aps.append(ap)
        rhs_s[hh, rows, :] = jnp.concatenate([v * beta, k * (beta * eg)], axis=1).astype(BF16)
        qg = (q * eg).astype(BF16)
        for bb in range(pack):
            wq_s[hh, p * pack + bb, chunk:2 * chunk, :] = qg[bb * chunk:(bb + 1) * chunk]
            chunk_decay[(hh, p * pack + bb)] = etot[bb * chunk:bb * chunk + 1, :]
        kd_s[hh, rows, :] = (k * erem).astype(BF16)
        qkm_s[hh, p] = (sims[pw:2 * pw] * decay).astype(BF16)

    tps = _packed_unit_lower_inverses(aps, INV_PASSES)

    for (hh, p), tp in zip(jobs, tps):
        rows = slice(p * pw, (p + 1) * pw)
        uw = _dot(_block_diag(tp.astype(BF16)), rhs_s[hh, rows, :])
        u_s[hh, rows, :] = uw[:, 0:hd]
        w = uw[:, hd:2 * hd].astype(BF16)
        for bb in range(pack):
            wq_s[hh, p * pack + bb, 0:chunk, :] = w[bb * chunk:(bb + 1) * chunk]

    for cidx in range(ts // chunk):
        rows = slice(cidx * chunk, (cidx + 1) * chunk)
        for hh in range(hg):
            st = state[hh]
            ws_qs = _dot(wq_s[hh, cidx], st.astype(BF16))
            v_new = (u_s[hh, rows, :] - ws_qs[0:chunk]).astype(BF16)
            vnew_s[hh, rows, :] = v_new
            qs_s[hh, rows, :] = ws_qs[chunk:2 * chunk]
            state[hh] = st * chunk_decay[(hh, cidx)] + _dot_tn(kd_s[hh, rows, :], v_new)

    for hh in range(hg):
        cols = slice(hh * hd, (hh + 1) * hd)
        for p in range(n_pack):
            rows = slice(p * pw, (p + 1) * pw)
            o = qs_s[hh, rows, :] + _dot(qkm_s[hh, p], vnew_s[hh, rows, :])
            zz = z_ref[rows, cols].astype(F32)
            o = o * lax.rsqrt(jnp.mean(o * o, axis=-1, keepdims=True) + EPS) * nw * _silu(zz)
            o_ref[rows, cols] = o.astype(o_ref.dtype)


def delta_net(proj, proj_z, gt, gr, conv_w, layer, norm_w, *, seq, q_off, k_off, v_off, z_off, n_heads):
    b = proj.shape[0]
    hd = HEAD_DIM
    hg = min(DN_HEADS_PER_STEP, n_heads)
    pw = DN_PACK * DN_CHUNK
    ts = _tile(seq, 512, align=pw)
    gw = hg * hd
    kw = n_heads * hd
    assert n_heads % hg == 0
    for off in (q_off, k_off, v_off, z_off, kw):
        assert off % gw == 0
    kern = functools.partial(_dn_kernel, ts=ts, hg=hg, chunk=DN_CHUNK)

    def col_spec(off):
        return pl.BlockSpec((None, ts, gw), lambda bi, g, s: (bi, s, off // gw + g))

    def w_spec(off):
        return pl.BlockSpec((None, DN_CONV, gw), lambda bi, g, s: (layer, 0, off // gw + g))

    scratch = ([pltpu.VMEM((V7X_SUBLANES, gw), F32)] * 3
               + [pltpu.VMEM((ts, gw), F32)] * 3
               + [pltpu.VMEM((hg, ts, 2 * hd), BF16),
                  pltpu.VMEM((hg, ts, hd), F32),
                  pltpu.VMEM((hg, ts // DN_CHUNK, 2 * DN_CHUNK, hd), BF16),
                  pltpu.VMEM((hg, ts, hd), BF16),
                  pltpu.VMEM((hg, ts // pw, pw, pw), BF16),
                  pltpu.VMEM((hg, ts, hd), BF16),
                  pltpu.VMEM((hg, ts, hd), F32),
                  pltpu.VMEM((hg, hd, hd), F32)])
    block_bytes = (5 * ts * gw * 2 + ts * V7X_LANES * 4 + (ts // pw) * n_heads * pw * 4
                   + 3 * (ts + V7X_SUBLANES) * gw * 4 + 3 * ts * gw * 4 + hg * ts * hd * 18)
    return pl.pallas_call(
        kern,
        grid=(b, n_heads // hg, seq // ts),
        in_specs=[col_spec(q_off), col_spec(k_off), col_spec(v_off), col_spec(z_off),
                  w_spec(0), w_spec(kw), w_spec(2 * kw),
                  pl.BlockSpec((None, ts, V7X_LANES), lambda bi, g, s: (bi, s, g)),
                  pl.BlockSpec((None, ts // pw, n_heads, pw), lambda bi, g, s: (bi, s, 0, 0)),
                  pl.BlockSpec((1, hd), lambda bi, g, s: (0, 0))],
        out_specs=pl.BlockSpec((None, ts, gw), lambda bi, g, s: (bi, s, g)),
        out_shape=jax.ShapeDtypeStruct((b, seq, kw), BF16),
        scratch_shapes=scratch,
        compiler_params=_params(("parallel", "parallel", "arbitrary"), block_bytes),
        name="delta_net",
    )(proj, proj, proj, proj_z, conv_w, conv_w, conv_w, gt, gr, norm_w.reshape(1, hd))


def _merge_kernel(oa_ref, od_ref, ga_ref, gd_ref, wa_ref, wd_ref, y_ref):
    a = _dot(oa_ref[...], wa_ref[...])
    sa = jax.nn.sigmoid(ga_ref[...].astype(F32))
    sd = jax.nn.sigmoid(gd_ref[...].astype(F32))
    d = _dot(od_ref[...], wd_ref[...])
    y_ref[...] = (sa * a + sd * d).astype(y_ref.dtype)


def branch_merge(o_a, o_d, proj, w_a, w_d, layer, *, ga_off, gd_off):
    t, ka = o_a.shape
    kd = o_d.shape[1]
    n = w_a.shape[2]
    tm = _tile(t, 1024)
    tn = _tile(n, 512)
    assert ga_off % tn == 0 and gd_off % tn == 0
    block_bytes = 2 * (tm * ka + tm * kd + 3 * tm * tn + ka * tn + kd * tn)
    return pl.pallas_call(
        _merge_kernel,
        grid=(t // tm, n // tn),
        in_specs=[pl.BlockSpec((tm, ka), lambda i, j: (i, 0)),
                  pl.BlockSpec((tm, kd), lambda i, j: (i, 0)),
                  pl.BlockSpec((tm, tn), lambda i, j: (i, ga_off // tn + j)),
                  pl.BlockSpec((tm, tn), lambda i, j: (i, gd_off // tn + j)),
                  pl.BlockSpec((None, ka, tn), lambda i, j: (layer, 0, j)),
                  pl.BlockSpec((None, kd, tn), lambda i, j: (layer, 0, j))],
        out_specs=pl.BlockSpec((tm, tn), lambda i, j: (i, j)),
        out_shape=jax.ShapeDtypeStruct((t, n), BF16),
        compiler_params=_params(("parallel", "parallel"), block_bytes),
        name="branch_merge",
    )(o_a, o_d, proj, proj, w_a, w_d)


def _proj_norm_res_kernel(*refs, with_next):
    if with_next:
        x_ref, w_ref, h_ref, post_ref, next_ref, h_out, xn_out = refs
    else:
        x_ref, w_ref, h_ref, post_ref, h_out = refs
    hn = h_ref[...] + _rms(_dot(x_ref[...], w_ref[...]), post_ref[...])
    h_out[...] = hn
    if with_next:
        xn_out[...] = _rms(hn, next_ref[...]).astype(xn_out.dtype)


def proj_norm_residual(x, w, layer, h, post_w, next_w):
    t, k = x.shape
    d = w.shape[2]
    tm = _tile(t, 256)
    with_next = next_w is not None
    kern = functools.partial(_proj_norm_res_kernel, with_next=with_next)
    row = pl.BlockSpec((1, d), lambda i: (0, 0))
    tile = pl.BlockSpec((tm, d), lambda i: (i, 0))
    in_specs = [pl.BlockSpec((tm, k), lambda i: (i, 0)),
                pl.BlockSpec((None, k, d), lambda i: (layer, 0, 0), pipeline_mode=pl.Buffered(1)), tile, row]
    args = [x, w, h, post_w.reshape(1, d)]
    out_specs = [tile]
    out_shape = [jax.ShapeDtypeStruct((t, d), F32)]
    if with_next:
        in_specs.append(row)
        args.append(next_w.reshape(1, d))
        out_specs.append(tile)
        out_shape.append(jax.ShapeDtypeStruct((t, d), BF16))
    block_bytes = tm * k * 2 + tm * d * (4 + 4 + 2) + (k * d * 2) // 2 + tm * d * 4
    res = pl.pallas_call(
        kern,
        grid=(t // tm,),
        in_specs=in_specs,
        out_specs=out_specs,
        out_shape=out_shape,
        compiler_params=pltpu.CompilerParams(
            dimension_semantics=("parallel",),
            vmem_limit_bytes=int(min(V7X_VMEM_BYTES - (2 << 20), 2 * block_bytes + (4 << 20)))),
        name="proj_norm_residual",
    )(*args)
    return (res[0], res[1]) if with_next else (res[0], None)


def _causal_conv(u, taps):
    y = u * taps[0:1, :]
    for j in range(1, taps.shape[0]):
        y = pltpu.roll(y, 1, 0) + u * taps[j:j + 1, :]
    return y


def _ffn_up_kernel(x_ref, wg_ref, wv_ref, cg_ref, cv_ref, bg_ref, bv_ref, o_ref, tail_g, tail_v,
                   *, tiles_per_seq, sub):
    i = pl.program_id(1)
    halo = V7X_SUBLANES
    tm, tn = o_ref.shape
    x = x_ref[...]
    blocks = [slice(lo, min(lo + sub, tn)) for lo in range(0, tn, sub)]

    @pl.when(i == 0)
    def _():
        tail_g[...] = jnp.zeros(tail_g.shape, F32)
        tail_v[...] = jnp.zeros(tail_v.shape, F32)

    def project(cols):
        return _dot(x, wg_ref[:, cols]), _dot(x, wv_ref[:, cols])

    def conv(u, tail, cols, c_ref, b_ref):
        hist = jnp.where((i % tiles_per_seq) != 0, tail[:, cols], 0.0)
        tail[:, cols] = u[tm - halo:, :]
        return _causal_conv(jnp.concatenate([hist, u], axis=0), c_ref[:, cols])[halo:] + b_ref[:, cols]

    pending = project(blocks[0])
    for s, cols in enumerate(blocks):
        ug, uv = pending
        if s + 1 < len(blocks):
            pending = project(blocks[s + 1])
        gate = conv(ug, tail_g, cols, cg_ref, bg_ref)
        val = conv(uv, tail_v, cols, cv_ref, bv_ref)
        o_ref[:, cols] = (_silu(gate) * val).astype(o_ref.dtype)


def ffn_up(xn, w_up, conv_w, conv_b, layer, *, seq):
    t, d = xn.shape
    dff = w_up.shape[2] // 2
    tm = _tile(seq, 1024)
    sub = 2 * V7X_LANES
    tn = _tile(dff, 2816)
    halo = V7X_SUBLANES
    nj = dff // tn
    kern = functools.partial(_ffn_up_kernel, tiles_per_seq=seq // tm, sub=sub)

    def w_spec(off):
        return pl.BlockSpec((None, d, tn), lambda j, i: (layer, 0, off + j), pipeline_mode=pl.Buffered(1))

    block_bytes = tm * d * 2 + tm * tn * 2 + d * tn * 2 + 2 * (tm + halo) * sub * 4
    return pl.pallas_call(
        kern,
        grid=(nj, t // tm),
        in_specs=[pl.BlockSpec((tm, d), lambda j, i: (i, 0)),
                  w_spec(0), w_spec(nj),
                  pl.BlockSpec((None, FFN_CONV, tn), lambda j, i: (layer, 0, j)),
                  pl.BlockSpec((None, FFN_CONV, tn), lambda j, i: (layer, 0, nj + j)),
                  pl.BlockSpec((None, 1, tn), lambda j, i: (layer, 0, j)),
                  pl.BlockSpec((None, 1, tn), lambda j, i: (layer, 0, nj + j))],
        out_specs=pl.BlockSpec((tm, tn), lambda j, i: (i, j)),
        out_shape=jax.ShapeDtypeStruct((t, dff), BF16),
        scratch_shapes=[pltpu.VMEM((halo, tn), F32)] * 2,
        compiler_params=_params(("arbitrary", "arbitrary"), block_bytes),
        name="ffn_up",
    )(xn, w_up, w_up, conv_w, conv_w, conv_b, conv_b)


def kernel(x, positions, norm_mix_pre, w_in, attn_sinks, dn_conv_w, dn_a_log, dn_dt_bias, dn_norm_w,
           w_branch_attn, w_branch_dn, w_out, norm_mix_post, norm_ffn_pre, w_ffn_up, ffn_conv_w,
           ffn_conv_b, w_ffn_down, norm_ffn_post):
    b, seq, d = x.shape
    depth = w_in.shape[0]
    t = b * seq
    n_attn = attn_sinks.shape[1]
    n_dn = dn_a_log.shape[1]
    qa_w = n_attn * HEAD_DIM
    kv_w = ATTN_KV_HEADS * HEAD_DIM
    dn_w = n_dn * HEAD_DIM

    edges = [0]
    for wdt in (qa_w, kv_w, kv_w, 3 * dn_w, n_dn, n_dn, dn_w, d, d):
        edges.append(edges[-1] + wdt)
    head_w = edges[4]
    tail_w = edges[9] - edges[6]
    w_head = w_in[:, :, :head_w].astype(BF16)
    w_tail = w_in[:, :, edges[6]:edges[9]].astype(BF16)
    dn_hg = min(DN_HEADS_PER_STEP, n_dn)
    w_gate = _gate_lanes(w_in[:, :, edges[4]:edges[5]], w_in[:, :, edges[5]:edges[6]], dn_hg).astype(BF16)
    q_off, k_off, v_off = 0, qa_w, qa_w + kv_w
    dq_off = qa_w + 2 * kv_w
    dk_off, dv_off = dq_off + dn_w, dq_off + 2 * dn_w
    z_off, ga_off, gd_off = 0, dn_w, dn_w + d

    w_ba = w_branch_attn.astype(BF16)
    w_bd = w_branch_dn.astype(BF16)
    w_o = w_out.astype(BF16)
    w_up = w_ffn_up.astype(BF16)
    w_dn = w_ffn_down.astype(BF16)
    ffn_b = ffn_conv_b.reshape(depth, 1, ffn_conv_b.shape[1])

    cos, sin = rope_tables(positions)
    cos = cos.reshape(b, seq, HEAD_DIM)
    sin = sin.reshape(b, seq, HEAD_DIM)

    h = x.reshape(t, d)
    xn = rmsnorm(h, norm_mix_pre[0])
    for l in range(depth):
        proj_head = matmul(xn, w_head, l, head_w)
        proj_tail = matmul(xn, w_tail, l, tail_w)
        head3 = proj_head.reshape(b, seq, head_w)
        o_a = attention(head3, cos, sin, attn_sinks[l], seq=seq, q_off=q_off, k_off=k_off,
                        v_off=v_off, n_heads=n_attn)
        gt, gr = dn_gates(xn, w_gate, l, dn_a_log[l], dn_dt_bias[l], hg=dn_hg)
        o_d = delta_net(head3, proj_tail.reshape(b, seq, tail_w), gt.reshape(b, seq, gt.shape[1]),
                        gr.reshape(b, seq // (DN_PACK * DN_CHUNK), n_dn, DN_PACK * DN_CHUNK),
                        dn_conv_w, l, dn_norm_w[l],
                        seq=seq, q_off=dq_off, k_off=dk_off, v_off=dv_off, z_off=z_off, n_heads=n_dn)
        y = branch_merge(o_a.reshape(t, qa_w), o_d.reshape(t, dn_w), proj_tail, w_ba, w_bd, l,
                         ga_off=ga_off, gd_off=gd_off)
        h, xf = proj_norm_residual(y, w_o, l, h, norm_mix_post[l], norm_ffn_pre[l])
        u = ffn_up(xf, w_up, ffn_conv_w, ffn_b, l, seq=seq)
        next_w = norm_mix_pre[l + 1] if l + 1 < depth else None
        h, xn = proj_norm_residual(u, w_dn, l, h, norm_ffn_post[l], next_w)
    return h.reshape(b, seq, d)
```

```python
import functools

import jax
import jax.numpy as jnp
from jax import lax
from jax.experimental import pallas as pl
from jax.experimental.pallas import tpu as pltpu

HEAD_DIM = 128
ATTN_KV_HEADS = 4
ATTN_BLOCK = 128
ROT_DIM = HEAD_DIM // 4
ROPE_THETA = 500000.0
DN_CHUNK = 64
DN_CONV = 4
FFN_CONV = 3
EPS = 1e-6
LOG2E = 1.4426950408889634

V7X_LANES = 128
V7X_SUBLANES = 8
V7X_VMEM_BYTES = 64 * 1024 * 1024

DN_PACK = 2 * V7X_LANES // DN_CHUNK
DN_HEADS_PER_STEP = 8
INV_PASSES = 1
INV_BASE = 16

F32 = jnp.float32
BF16 = jnp.bfloat16


def _dot(a, b):
    return jnp.dot(a, b, preferred_element_type=F32)


def _dot_nt(a, b):
    return lax.dot_general(a, b, (((1,), (1,)), ((), ())), preferred_element_type=F32)


def _dot_tn(a, b):
    return lax.dot_general(a, b, (((0,), (0,)), ((), ())), preferred_element_type=F32)


def _rms(x, w):
    return x * lax.rsqrt(jnp.mean(x * x, axis=-1, keepdims=True) + EPS) * w


def _silu(x):
    h = 0.5 * x
    return h + h * jnp.tanh(h)


def _vmem_limit(block_bytes):
    return int(min(V7X_VMEM_BYTES - (2 << 20), 2 * block_bytes + (12 << 20)))


def _params(semantics, block_bytes):
    return pltpu.CompilerParams(dimension_semantics=semantics,
                                vmem_limit_bytes=_vmem_limit(block_bytes))


def _tile(n, want, align=V7X_LANES):
    for t in range(min(n, want) // align * align, 0, -align):
        if n % t == 0:
            return t
    raise ValueError(f"no {align}-aligned tile divides {n}")


def _rmsnorm_kernel(x_ref, w_ref, o_ref):
    o_ref[...] = _rms(x_ref[...], w_ref[...]).astype(o_ref.dtype)


def rmsnorm(x, w):
    t, d = x.shape
    tm = _tile(t, 512)
    return pl.pallas_call(
        _rmsnorm_kernel,
        grid=(t // tm,),
        in_specs=[pl.BlockSpec((tm, d), lambda i: (i, 0)),
                  pl.BlockSpec((1, d), lambda i: (0, 0))],
        out_specs=pl.BlockSpec((tm, d), lambda i: (i, 0)),
        out_shape=jax.ShapeDtypeStruct((t, d), BF16),
        compiler_params=_params(("parallel",), tm * d * 6),
        name="rmsnorm",
    )(x, w.reshape(1, d))


def _matmul_kernel(x_ref, w_ref, o_ref):
    o_ref[...] = _dot(x_ref[...], w_ref[...]).astype(o_ref.dtype)


def matmul(x, w, layer, n):
    t, k = x.shape
    tm = _tile(t, 1024)
    tn = _tile(n, 1024)
    return pl.pallas_call(
        _matmul_kernel,
        grid=(t // tm, n // tn),
        in_specs=[pl.BlockSpec((tm, k), lambda i, j: (i, 0)),
                  pl.BlockSpec((None, k, tn), lambda i, j: (layer, 0, j))],
        out_specs=pl.BlockSpec((tm, tn), lambda i, j: (i, j)),
        out_shape=jax.ShapeDtypeStruct((t, n), BF16),
        compiler_params=_params(("parallel", "parallel"), 2 * (tm * k + k * tn + tm * tn)),
        name="in_proj",
    )(x, w)


def _rope_table_kernel(pos_ref, freq_ref, sign_ref, cos_ref, sin_ref):
    ang = pos_ref[...] * freq_ref[...]
    cos_ref[...] = jnp.cos(ang)
    sin_ref[...] = jnp.sin(ang) * sign_ref[...]


def rope_tables(positions):
    t = positions.size
    half = ROT_DIM // 2
    inv_freq = ROPE_THETA ** (-jnp.arange(0, ROT_DIM, 2, dtype=F32) / ROT_DIM)
    freq = jnp.concatenate([inv_freq, inv_freq, jnp.zeros((HEAD_DIM - ROT_DIM,), F32)]).reshape(1, HEAD_DIM)
    sign = jnp.concatenate([-jnp.ones((half,), F32), jnp.ones((HEAD_DIM - half,), F32)]).reshape(1, HEAD_DIM)
    pos = positions.astype(F32).reshape(t, 1)
    tm = _tile(t, 1024)
    row = pl.BlockSpec((1, HEAD_DIM), lambda i: (0, 0))
    out = pl.BlockSpec((tm, HEAD_DIM), lambda i: (i, 0))
    return pl.pallas_call(
        _rope_table_kernel,
        grid=(t // tm,),
        in_specs=[pl.BlockSpec((tm, 1), lambda i: (i, 0)), row, row],
        out_specs=[out, out],
        out_shape=[jax.ShapeDtypeStruct((t, HEAD_DIM), F32)] * 2,
        compiler_params=_params(("parallel",), 3 * tm * HEAD_DIM * 4),
        name="rope_tables",
    )(pos, freq, sign)


def _rotate_half_matrix():
    half = ROT_DIM // 2
    r = jnp.arange(HEAD_DIM)[:, None]
    c = jnp.arange(HEAD_DIM)[None, :]
    p = ((c < half) & (r == c + half)) | ((c >= half) & (c < ROT_DIM) & (r == c - half))
    return p.astype(BF16)


def _attn_kernel(sink_ref, q_ref, k_ref, v_ref, cos_ref, sin_ref, perm_ref, o_ref, kbuf, vbuf,
                 *, tq, n_kv, group):
    i = pl.program_id(1)
    blk = ATTN_BLOCK
    hd = HEAD_DIM

    @pl.when(i == 0)
    def _():
        kbuf[0:blk, :] = jnp.zeros((blk, n_kv * hd), kbuf.dtype)
        vbuf[0:blk, :] = jnp.zeros((blk, n_kv * hd), vbuf.dtype)

    @pl.when(i > 0)
    def _():
        kbuf[0:blk, :] = kbuf[tq:tq + blk, :]
        vbuf[0:blk, :] = vbuf[tq:tq + blk, :]

    perm = perm_ref[...]

    def rope(xb, c, s):
        return (xb.astype(F32) * c + _dot(xb, perm) * s).astype(BF16)

    cos = cos_ref[...]
    sin = sin_ref[...]
    for kh in range(n_kv):
        cols = slice(kh * hd, (kh + 1) * hd)
        kbuf[blk:blk + tq, cols] = rope(k_ref[:, cols], cos, sin)
    vbuf[blk:blk + tq, :] = v_ref[...]

    c = lax.broadcasted_iota(jnp.int32, (2 * blk, group * blk), 0)
    r = lax.broadcasted_iota(jnp.int32, (2 * blk, group * blk), 1) % blk
    band = (c > r) & (c <= r + blk)
    band_first = band & (c >= blk)
    scale = hd ** -0.5 * LOG2E
    kv_heads = range(n_kv)

    def head_ids(kh):
        return [kh * group + g for g in range(group)]

    for j in range(tq // blk):
        rows = slice(j * blk, (j + 1) * blk)
        keys = slice(j * blk, (j + 2) * blk)
        cj = cos[rows]
        sj = sin[rows]
        mask = band if j > 0 else (band_first | (band & (i > 0)))
        qs = [jnp.concatenate([rope(q_ref[rows, h * hd:(h + 1) * hd], cj, sj) for h in head_ids(kh)], axis=0)
              for kh in kv_heads]
        logits = [jnp.where(mask, _dot_nt(kbuf[keys, kh * hd:(kh + 1) * hd], qs[kh]) * scale, -jnp.inf)
                  for kh in kv_heads]
        sinks = [jnp.concatenate([jnp.full((1, blk), sink_ref[h] * LOG2E, F32) for h in head_ids(kh)], axis=1)
                 for kh in kv_heads]
        ms = [jnp.maximum(jnp.max(logits[kh], axis=0, keepdims=True), sinks[kh]) for kh in kv_heads]
        ps = [jnp.exp2(logits[kh] - ms[kh]) for kh in kv_heads]
        denoms = [jnp.sum(ps[kh], axis=0, keepdims=True) + jnp.exp2(sinks[kh] - ms[kh]) for kh in kv_heads]
        outs = [(_dot_tn(vbuf[keys, kh * hd:(kh + 1) * hd], ps[kh].astype(BF16)) / denoms[kh]).T
                for kh in kv_heads]
        for kh in kv_heads:
            for g, h in enumerate(head_ids(kh)):
                o_ref[rows, h * hd:(h + 1) * hd] = outs[kh][g * blk:(g + 1) * blk].astype(o_ref.dtype)


def attention(proj, cos, sin, sinks, *, seq, q_off, k_off, v_off, n_heads):
    b = proj.shape[0]
    n_kv = ATTN_KV_HEADS
    group = n_heads // n_kv
    qw = n_heads * HEAD_DIM
    kw = n_kv * HEAD_DIM
    tq = _tile(seq, 512)
    assert q_off % qw == 0 and k_off % kw == 0 and v_off % kw == 0
    kern = functools.partial(_attn_kernel, tq=tq, n_kv=n_kv, group=group)
    block_bytes = tq * (2 * qw + 2 * kw) * 2 + 2 * tq * HEAD_DIM * 4 + 2 * (tq + ATTN_BLOCK) * kw * 2
    return pl.pallas_call(
        kern,
        grid=(b, seq // tq),
        in_specs=[pl.BlockSpec(memory_space=pltpu.SMEM),
                  pl.BlockSpec((None, tq, qw), lambda bi, i: (bi, i, q_off // qw)),
                  pl.BlockSpec((None, tq, kw), lambda bi, i: (bi, i, k_off // kw)),
                  pl.BlockSpec((None, tq, kw), lambda bi, i: (bi, i, v_off // kw)),
                  pl.BlockSpec((None, tq, HEAD_DIM), lambda bi, i: (bi, i, 0)),
                  pl.BlockSpec((None, tq, HEAD_DIM), lambda bi, i: (bi, i, 0)),
                  pl.BlockSpec((HEAD_DIM, HEAD_DIM), lambda bi, i: (0, 0))],
        out_specs=pl.BlockSpec((None, tq, qw), lambda bi, i: (bi, i, 0)),
        out_shape=jax.ShapeDtypeStruct((b, seq, qw), BF16),
        scratch_shapes=[pltpu.VMEM((tq + ATTN_BLOCK, kw), BF16),
                        pltpu.VMEM((tq + ATTN_BLOCK, kw), BF16)],
        compiler_params=_params(("parallel", "arbitrary"), block_bytes),
        name="swa_attention",
    )(sinks, proj, proj, proj, cos, sin, _rotate_half_matrix())


GATE_GROUPS = 5


def _gate_lanes(b_part, a_part, hg):
    n_heads = a_part.shape[-1]
    assert n_heads % hg == 0 and GATE_GROUPS * hg <= V7X_LANES
    pad = jnp.zeros(a_part.shape[:-1] + (V7X_LANES - GATE_GROUPS * hg,), a_part.dtype)
    parts = []
    for g in range(n_heads // hg):
        heads = slice(g * hg, (g + 1) * hg)
        parts += [b_part[..., heads]] + [a_part[..., heads]] * (GATE_GROUPS - 1) + [pad]
    return jnp.concatenate(parts, axis=-1)


def _gates_kernel(x_ref, w_ref, alog_ref, dt_ref, gt_ref, gr_ref, *, hg, chunk):
    tm = x_ref.shape[0]
    pw = DN_PACK * chunk
    val = _dot(x_ref[...], w_ref[...])
    quantity = (lax.broadcasted_iota(jnp.int32, val.shape, 1) % V7X_LANES) // hg
    beta = jax.nn.sigmoid(val)
    z = val + dt_ref[...]
    softplus = jnp.maximum(z, 0.0) + jnp.log1p(jnp.exp(-jnp.abs(z)))
    g = jnp.where((quantity >= 1) & (quantity < GATE_GROUPS), -jnp.exp(alog_ref[...]) * softplus, 0.0)
    r = lax.broadcasted_iota(jnp.int32, (tm, tm), 0)
    c = lax.broadcasted_iota(jnp.int32, (tm, tm), 1)
    tri = (((r // chunk) == (c // chunk)) & (c <= r)).astype(BF16)
    hi = g.astype(BF16)
    rem = g - hi.astype(F32)
    mid = rem.astype(BF16)
    lo = (rem - mid.astype(F32)).astype(BF16)
    gc = _dot(tri, hi) + _dot(tri, mid) + _dot(tri, lo)
    total = jnp.concatenate(
        [jnp.broadcast_to(gc[(ci + 1) * chunk - 1:(ci + 1) * chunk, :], (chunk, gc.shape[1]))
         for ci in range(tm // chunk)], axis=0)
    gt_ref[...] = jnp.where(quantity == 0, beta,
                  jnp.where(quantity == 1, gc,
                  jnp.where(quantity == 2, jnp.exp(gc),
                  jnp.where(quantity == 3, jnp.exp(total - gc),
                  jnp.where(quantity == 4, jnp.exp(total), 0.0)))))
    gct = gc.T
    for pi in range(tm // pw):
        for grp in range(gc.shape[1] // V7X_LANES):
            lo_row = grp * V7X_LANES + hg
            gr_ref[pi, grp * hg:(grp + 1) * hg, :] = gct[lo_row:lo_row + hg, pi * pw:(pi + 1) * pw]


def dn_gates(xn, w_gate, layer, a_log, dt_bias, *, hg):
    t, d = xn.shape
    n_heads = a_log.shape[0]
    gl = w_gate.shape[2]
    pw = DN_PACK * DN_CHUNK
    tm = _tile(t, pw, align=pw)
    alog = _gate_lanes(jnp.zeros_like(a_log), a_log, hg).reshape(1, gl)
    dt = _gate_lanes(jnp.zeros_like(dt_bias), dt_bias, hg).reshape(1, gl)
    row = pl.BlockSpec((1, gl), lambda i: (0, 0))
    kern = functools.partial(_gates_kernel, hg=hg, chunk=DN_CHUNK)
    return pl.pallas_call(
        kern,
        grid=(t // tm,),
        in_specs=[pl.BlockSpec((tm, d), lambda i: (i, 0)),
                  pl.BlockSpec((None, d, gl), lambda i: (layer, 0, 0)), row, row],
        out_specs=[pl.BlockSpec((tm, gl), lambda i: (i, 0)),
                   pl.BlockSpec((tm // pw, n_heads, pw), lambda i: (i, 0, 0))],
        out_shape=[jax.ShapeDtypeStruct((t, gl), F32),
                   jax.ShapeDtypeStruct((t // pw, n_heads, pw), F32)],
        compiler_params=_params(("parallel",), tm * d * 2 + d * gl * 2 + 4 * tm * gl * 4 + 4 * tm * tm),
        name="dn_gates",
    )(xn, w_gate, alog, dt)


def _block_diag(xp):
    n, m = xp.shape
    tiled = jnp.concatenate([xp] * (m // n), axis=0)
    r = lax.broadcasted_iota(jnp.int32, (m, m), 0) // n
    c = lax.broadcasted_iota(jnp.int32, (m, m), 1) // n
    return jnp.where(r == c, tiled, jnp.zeros_like(tiled))


def _packed_dot(xp, yp, passes):
    xh = xp.astype(BF16)
    yh = yp.astype(BF16)
    out = _dot(xh, _block_diag(yh))
    if passes >= 3:
        xl = (xp - xh.astype(F32)).astype(BF16)
        yl = (yp - yh.astype(F32)).astype(BF16)
        out = out + _dot(xh, _block_diag(yl)) + _dot(xl, _block_diag(yh))
    return out


def _packed_unit_lower_inverses(aps, passes):
    n = aps[0].shape[0]
    r = lax.broadcasted_iota(jnp.int32, aps[0].shape, 0)
    c = lax.broadcasted_iota(jnp.int32, aps[0].shape, 1) % n
    eye = (r == c).astype(F32)
    size = INV_BASE
    in_blk = (r // size) == (c // size)
    xs = [jnp.where(in_blk, -ap, 0.0) for ap in aps]
    invs = [eye + x for x in xs]
    power = 1
    while 2 * power < size:
        xs = [_packed_dot(x, x, passes) for x in xs]
        invs = [inv + _packed_dot(inv, x, passes) for inv, x in zip(invs, xs)]
        power *= 2
    while size < n:
        merged = (r // (2 * size)) == (c // (2 * size))
        offs = [_packed_dot(jnp.where(merged & ~in_blk, ap, 0.0), inv, passes) for ap, inv in zip(aps, invs)]
        invs = [inv - _packed_dot(inv, off, passes) for inv, off in zip(invs, offs)]
        in_blk = merged
        size *= 2
    return invs


def _dn_kernel(q_ref, k_ref, v_ref, z_ref, wq_ref, wk_ref, wv_ref, gt_ref, gr_ref, nw_ref, o_ref,
               qtail, ktail, vtail, qn, kn, vn, rhs_s, u_s, wq_s, kd_s, qkm_s, vnew_s, qs_s, state,
               *, ts, hg, chunk):
    s = pl.program_id(2)
    gi = pl.program_id(1)
    hd = HEAD_DIM
    halo = V7X_SUBLANES

    @pl.when(s == 0)
    def _():
        state[...] = jnp.zeros(state.shape, F32)
        for tail in (qtail, ktail, vtail):
            tail[...] = jnp.zeros(tail.shape, F32)

    def conv_silu(src_ref, tail, w_ref):
        x = src_ref[...].astype(F32)
        xe = jnp.concatenate([tail[...], x], axis=0)
        tail[...] = x[ts - halo:, :]
        return _silu(_causal_conv(xe, w_ref[...])[halo:])

    def l2norm(x, scale):
        return x * (lax.rsqrt(jnp.sum(x * x, axis=-1, keepdims=True) + EPS) * scale)

    qc = conv_silu(q_ref, qtail, wq_ref)
    kc = conv_silu(k_ref, ktail, wk_ref)
    vn[...] = conv_silu(v_ref, vtail, wv_ref)
    for hh in range(hg):
        cols = slice(hh * hd, (hh + 1) * hd)
        qn[:, cols] = l2norm(qc[:, cols], hd ** -0.5)
        kn[:, cols] = l2norm(kc[:, cols], 1.0)

    pack = DN_PACK
    pw = pack * chunk
    n_pack = ts // pw
    r = lax.broadcasted_iota(jnp.int32, (pw, pw), 0)
    c = lax.broadcasted_iota(jnp.int32, (pw, pw), 1)
    same = (r // chunk) == (c // chunk)
    tri = same & (r >= c)
    strict = same & (r > c)
    lane = lax.broadcasted_iota(jnp.int32, (pw, V7X_LANES), 1)
    nw = nw_ref[...]

    jobs = [(hh, p) for hh in range(hg) for p in range(n_pack)]
    chunk_decay = {}
    aps = []
    for hh, p in jobs:
        cols = slice(hh * hd, (hh + 1) * hd)
        rows = slice(p * pw, (p + 1) * pw)
        gt = gt_ref[rows, :]
        beta, gc, eg, erem, etot = (jnp.sum(jnp.where(lane == qi * hg + hh, gt, 0.0), axis=-1, keepdims=True)
                                    for qi in range(GATE_GROUPS))
        gc_row = gr_ref[p, pl.ds(gi * hg + hh, 1), :]
        q = qn[rows, cols]
        k = kn[rows, cols]
        v = vn[rows, cols]
        kb = k.astype(BF16)
        sims = _dot_nt(jnp.concatenate([kb, q.astype(BF16)], axis=0), kb)
        decay = jnp.exp(jnp.where(tri, gc - gc_row, -jnp.inf))
        a_full = jnp.where(strict, beta * sims[0:pw] * decay, 0.0)
        ap = a_full[0:chunk]
        for bb in range(1, pack):
            ap = ap + a_full[bb * chunk:(bb + 1) * chunk]
        aps.append(ap)
        rhs_s[hh, rows, :] = jnp.concatenate([v * beta, k * (beta * eg)], axis=1).astype(BF16)
        qg = (q * eg).astype(BF16)
        for bb in range(pack):
            wq_s[hh, p * pack + bb, chunk:2 * chunk, :] = qg[bb * chunk:(bb + 1) * chunk]
            chunk_decay[(hh, p * pack + bb)] = etot[bb * chunk:bb * chunk + 1, :]
        kd_s[hh, rows, :] = (k * erem).astype(BF16)
        qkm_s[hh, p] = (sims[pw:2 * pw] * decay).astype(BF16)

    tps = _packed_unit_lower_inverses(aps, INV_PASSES)

    for (hh, p), tp in zip(jobs, tps):
        rows = slice(p * pw, (p + 1) * pw)
        uw = _dot(_block_diag(tp.astype(BF16)), rhs_s[hh, rows, :])
        u_s[hh, rows, :] = uw[:, 0:hd]
        w = uw[:, hd:2 * hd].astype(BF16)
        for bb in range(pack):
            wq_s[hh, p * pack + bb, 0:chunk, :] = w[bb * chunk:(bb + 1) * chunk]

    for cidx in range(ts // chunk):
        rows = slice(cidx * chunk, (cidx + 1) * chunk)
        for hh in range(hg):
            st = state[hh]
            ws_qs = _dot(wq_s[hh, cidx], st.astype(BF16))
            v_new = (u_s[hh, rows, :] - ws_qs[0:chunk]).astype(BF16)
            vnew_s[hh, rows, :] = v_new
            qs_s[hh, rows, :] = ws_qs[chunk:2 * chunk]
            state[hh] = st * chunk_decay[(hh, cidx)] + _dot_tn(kd_s[hh, rows, :], v_new)

    for hh in range(hg):
        cols = slice(hh * hd, (hh + 1) * hd)
        for p in range(n_pack):
            rows = slice(p * pw, (p + 1) * pw)
            o = qs_s[hh, rows, :] + _dot(qkm_s[hh, p], vnew_s[hh, rows, :])
            zz = z_ref[rows, cols].astype(F32)
            o = o * lax.rsqrt(jnp.mean(o * o, axis=-1, keepdims=True) + EPS) * nw * _silu(zz)
            o_ref[rows, cols] = o.astype(o_ref.dtype)


def delta_net(proj, proj_z, gt, gr, conv_w, layer, norm_w, *, seq, q_off, k_off, v_off, z_off, n_heads):
    b = proj.shape[0]
    hd = HEAD_DIM
    hg = min(DN_HEADS_PER_STEP, n_heads)
    pw = DN_PACK * DN_CHUNK
    ts = _tile(seq, 512, align=pw)
    gw = hg * hd
    kw = n_heads * hd
    assert n_heads % hg == 0
    for off in (q_off, k_off, v_off, z_off, kw):
        assert off % gw == 0
    kern = functools.partial(_dn_kernel, ts=ts, hg=hg, chunk=DN_CHUNK)

    def col_spec(off):
        return pl.BlockSpec((None, ts, gw), lambda bi, g, s: (bi, s, off // gw + g))

    def w_spec(off):
        return pl.BlockSpec((None, DN_CONV, gw), lambda bi, g, s: (layer, 0, off // gw + g))

    scratch = ([pltpu.VMEM((V7X_SUBLANES, gw), F32)] * 3
               + [pltpu.VMEM((ts, gw), F32)] * 3
               + [pltpu.VMEM((hg, ts, 2 * hd), BF16),
                  pltpu.VMEM((hg, ts, hd), F32),
                  pltpu.VMEM((hg, ts // DN_CHUNK, 2 * DN_CHUNK, hd), BF16),
                  pltpu.VMEM((hg, ts, hd), BF16),
                  pltpu.VMEM((hg, ts // pw, pw, pw), BF16),
                  pltpu.VMEM((hg, ts, hd), BF16),
                  pltpu.VMEM((hg, ts, hd), F32),
                  pltpu.VMEM((hg, hd, hd), F32)])
    block_bytes = (5 * ts * gw * 2 + ts * V7X_LANES * 4 + (ts // pw) * n_heads * pw * 4
                   + 3 * (ts + V7X_SUBLANES) * gw * 4 + 3 * ts * gw * 4 + hg * ts * hd * 18)
    return pl.pallas_call(
        kern,
        grid=(b, n_heads // hg, seq // ts),
        in_specs=[col_spec(q_off), col_spec(k_off), col_spec(v_off), col_spec(z_off),
                  w_spec(0), w_spec(kw), w_spec(2 * kw),
                  pl.BlockSpec((None, ts, V7X_LANES), lambda bi, g, s: (bi, s, g)),
                  pl.BlockSpec((None, ts // pw, n_heads, pw), lambda bi, g, s: (bi, s, 0, 0)),
                  pl.BlockSpec((1, hd), lambda bi, g, s: (0, 0))],
        out_specs=pl.BlockSpec((None, ts, gw), lambda bi, g, s: (bi, s, g)),
        out_shape=jax.ShapeDtypeStruct((b, seq, kw), BF16),
        scratch_shapes=scratch,
        compiler_params=_params(("parallel", "parallel", "arbitrary"), block_bytes),
        name="delta_net",
    )(proj, proj, proj, proj_z, conv_w, conv_w, conv_w, gt, gr, norm_w.reshape(1, hd))


def _merge_kernel(oa_ref, od_ref, ga_ref, gd_ref, wa_ref, wd_ref, y_ref):
    a = _dot(oa_ref[...], wa_ref[...])
    sa = jax.nn.sigmoid(ga_ref[...].astype(F32))
    sd = jax.nn.sigmoid(gd_ref[...].astype(F32))
    d = _dot(od_ref[...], wd_ref[...])
    y_ref[...] = (sa * a + sd * d).astype(y_ref.dtype)


def branch_merge(o_a, o_d, proj, w_a, w_d, layer, *, ga_off, gd_off):
    t, ka = o_a.shape
    kd = o_d.shape[1]
    n = w_a.shape[2]
    tm = _tile(t, 1024)
    tn = _tile(n, 512)
    assert ga_off % tn == 0 and gd_off % tn == 0
    block_bytes = 2 * (tm * ka + tm * kd + 3 * tm * tn + ka * tn + kd * tn)
    return pl.pallas_call(
        _merge_kernel,
        grid=(t // tm, n // tn),
        in_specs=[pl.BlockSpec((tm, ka), lambda i, j: (i, 0)),
                  pl.BlockSpec((tm, kd), lambda i, j: (i, 0)),
                  pl.BlockSpec((tm, tn), lambda i, j: (i, ga_off // tn + j)),
                  pl.BlockSpec((tm, tn), lambda i, j: (i, gd_off // tn + j)),
                  pl.BlockSpec((None, ka, tn), lambda i, j: (layer, 0, j)),
                  pl.BlockSpec((None, kd, tn), lambda i, j: (layer, 0, j))],
        out_specs=pl.BlockSpec((tm, tn), lambda i, j: (i, j)),
        out_shape=jax.ShapeDtypeStruct((t, n), BF16),
        compiler_params=_params(("parallel", "parallel"), block_bytes),
        name="branch_merge",
    )(o_a, o_d, proj, proj, w_a, w_d)


def _proj_norm_res_kernel(*refs, with_next):
    if with_next:
        x_ref, w_ref, h_ref, post_ref, next_ref, h_out, xn_out = refs
    else:
        x_ref, w_ref, h_ref, post_ref, h_out = refs
    hn = h_ref[...] + _rms(_dot(x_ref[...], w_ref[...]), post_ref[...])
    h_out[...] = hn
    if with_next:
        xn_out[...] = _rms(hn, next_ref[...]).astype(xn_out.dtype)


def proj_norm_residual(x, w, layer, h, post_w, next_w):
    t, k = x.shape
    d = w.shape[2]
    tm = _tile(t, 256)
    with_next = next_w is not None
    kern = functools.partial(_proj_norm_res_kernel, with_next=with_next)
    row = pl.BlockSpec((1, d), lambda i: (0, 0))
    tile = pl.BlockSpec((tm, d), lambda i: (i, 0))
    in_specs = [pl.BlockSpec((tm, k), lambda i: (i, 0)),
                pl.BlockSpec((None, k, d), lambda i: (layer, 0, 0), pipeline_mode=pl.Buffered(1)), tile, row]
    args = [x, w, h, post_w.reshape(1, d)]
    out_specs = [tile]
    out_shape = [jax.ShapeDtypeStruct((t, d), F32)]
    if with_next:
        in_specs.append(row)
        args.append(next_w.reshape(1, d))
        out_specs.append(tile)
        out_shape.append(jax.ShapeDtypeStruct((t, d), BF16))
    block_bytes = tm * k * 2 + tm * d * (4 + 4 + 2) + (k * d * 2) // 2 + tm * d * 4
    res = pl.pallas_call(
        kern,
        grid=(t // tm,),
        in_specs=in_specs,
        out_specs=out_specs,
        out_shape=out_shape,
        compiler_params=pltpu.CompilerParams(
            dimension_semantics=("parallel",),
            vmem_limit_bytes=int(min(V7X_VMEM_BYTES - (2 << 20), 2 * block_bytes + (4 << 20)))),
        name="proj_norm_residual",
    )(*args)
    return (res[0], res[1]) if with_next else (res[0], None)


def _causal_conv(u, taps):
    y = u * taps[0:1, :]
    for j in range(1, taps.shape[0]):
        y = pltpu.roll(y, 1, 0) + u * taps[j:j + 1, :]
    return y


def _ffn_up_kernel(x_ref, wg_ref, wv_ref, cg_ref, cv_ref, bg_ref, bv_ref, o_ref, tail_g, tail_v,
                   *, tiles_per_seq, sub):
    i = pl.program_id(1)
    halo = V7X_SUBLANES
    tm, tn = o_ref.shape
    x = x_ref[...]
    blocks = [slice(lo, min(lo + sub, tn)) for lo in range(0, tn, sub)]

    @pl.when(i == 0)
    def _():
        tail_g[...] = jnp.zeros(tail_g.shape, F32)
        tail_v[...] = jnp.zeros(tail_v.shape, F32)

    def project(cols):
        return _dot(x, wg_ref[:, cols]), _dot(x, wv_ref[:, cols])

    def conv(u, tail, cols, c_ref, b_ref):
        hist = jnp.where((i % tiles_per_seq) != 0, tail[:, cols], 0.0)
        tail[:, cols] = u[tm - halo:, :]
        return _causal_conv(jnp.concatenate([hist, u], axis=0), c_ref[:, cols])[halo:] + b_ref[:, cols]

    pending = project(blocks[0])
    for s, cols in enumerate(blocks):
        ug, uv = pending
        if s + 1 < len(blocks):
            pending = project(blocks[s + 1])
        gate = conv(ug, tail_g, cols, cg_ref, bg_ref)
        val = conv(uv, tail_v, cols, cv_ref, bv_ref)
        o_ref[:, cols] = (_silu(gate) * val).astype(o_ref.dtype)


def ffn_up(xn, w_up, conv_w, conv_b, layer, *, seq):
    t, d = xn.shape
    dff = w_up.shape[2] // 2
    tm = _tile(seq, 1024)
    sub = 4 * V7X_LANES
    tn = _tile(dff, 2816)
    halo = V7X_SUBLANES
    nj = dff // tn
    kern = functools.partial(_ffn_up_kernel, tiles_per_seq=seq // tm, sub=sub)

    def w_spec(off):
        return pl.BlockSpec((None, d, tn), lambda j, i: (layer, 0, off + j), pipeline_mode=pl.Buffered(1))

    block_bytes = tm * d * 2 + tm * tn * 2 + d * tn * 2 + 2 * (tm + halo) * sub * 4
    return pl.pallas_call(
        kern,
        grid=(nj, t // tm),
        in_specs=[pl.BlockSpec((tm, d), lambda j, i: (i, 0)),
                  w_spec(0), w_spec(nj),
                  pl.BlockSpec((None, FFN_CONV, tn), lambda j, i: (layer, 0, j)),
                  pl.BlockSpec((None, FFN_CONV, tn), lambda j, i: (layer, 0, nj + j)),
                  pl.BlockSpec((None, 1, tn), lambda j, i: (layer, 0, j)),
                  pl.BlockSpec((None, 1, tn), lambda j, i: (layer, 0, nj + j))],
        out_specs=pl.BlockSpec((tm, tn), lambda j, i: (i, j)),
        out_shape=jax.ShapeDtypeStruct((t, dff), BF16),
        scratch_shapes=[pltpu.VMEM((halo, tn), F32)] * 2,
        compiler_params=_params(("arbitrary", "arbitrary"), block_bytes),
        name="ffn_up",
    )(xn, w_up, w_up, conv_w, conv_w, conv_b, conv_b)


def kernel(x, positions, norm_mix_pre, w_in, attn_sinks, dn_conv_w, dn_a_log, dn_dt_bias, dn_norm_w,
           w_branch_attn, w_branch_dn, w_out, norm_mix_post, norm_ffn_pre, w_ffn_up, ffn_conv_w,
           ffn_conv_b, w_ffn_down, norm_ffn_post):
    b, seq, d = x.shape
    depth = w_in.shape[0]
    t = b * seq
    n_attn = attn_sinks.shape[1]
    n_dn = dn_a_log.shape[1]
    qa_w = n_attn * HEAD_DIM
    kv_w = ATTN_KV_HEADS * HEAD_DIM
    dn_w = n_dn * HEAD_DIM

    edges = [0]
    for wdt in (qa_w, kv_w, kv_w, 3 * dn_w, n_dn, n_dn, dn_w, d, d):
        edges.append(edges[-1] + wdt)
    head_w = edges[4]
    tail_w = edges[9] - edges[6]
    w_head = w_in[:, :, :head_w].astype(BF16)
    w_tail = w_in[:, :, edges[6]:edges[9]].astype(BF16)
    dn_hg = min(DN_HEADS_PER_STEP, n_dn)
    w_gate = _gate_lanes(w_in[:, :, edges[4]:edges[5]], w_in[:, :, edges[5]:edges[6]], dn_hg).astype(BF16)
    q_off, k_off, v_off = 0, qa_w, qa_w + kv_w
    dq_off = qa_w + 2 * kv_w
    dk_off, dv_off = dq_off + dn_w, dq_off + 2 * dn_w
    z_off, ga_off, gd_off = 0, dn_w, dn_w + d

    w_ba = w_branch_attn.astype(BF16)
    w_bd = w_branch_dn.astype(BF16)
    w_o = w_out.astype(BF16)
    w_up = w_ffn_up.astype(BF16)
    w_dn = w_ffn_down.astype(BF16)
    ffn_b = ffn_conv_b.reshape(depth, 1, ffn_conv_b.shape[1])

    cos, sin = rope_tables(positions)
    cos = cos.reshape(b, seq, HEAD_DIM)
    sin = sin.reshape(b, seq, HEAD_DIM)

    h = x.reshape(t, d)
    xn = rmsnorm(h, norm_mix_pre[0])
    for l in range(depth):
        proj_head = matmul(xn, w_head, l, head_w)
        proj_tail = matmul(xn, w_tail, l, tail_w)
        head3 = proj_head.reshape(b, seq, head_w)
        o_a = attention(head3, cos, sin, attn_sinks[l], seq=seq, q_off=q_off, k_off=k_off,
                        v_off=v_off, n_heads=n_attn)
        gt, gr = dn_gates(xn, w_gate, l, dn_a_log[l], dn_dt_bias[l], hg=dn_hg)
        o_d = delta_net(head3, proj_tail.reshape(b, seq, tail_w), gt.reshape(b, seq, gt.shape[1]),
                        gr.reshape(b, seq // (DN_PACK * DN_CHUNK), n_dn, DN_PACK * DN_CHUNK),
                        dn_conv_w, l, dn_norm_w[l],
                        seq=seq, q_off=dq_off, k_off=dk_off, v_off=dv_off, z_off=z_off, n_heads=n_dn)
        y = branch_merge(o_a.reshape(t, qa_w), o_d.reshape(t, dn_w), proj_tail, w_ba, w_bd, l,
                         ga_off=ga_off, gd_off=gd_off)
        h, xf = proj_norm_residual(y, w_o, l, h, norm_mix_post[l], norm_ffn_pre[l])
        u = ffn_up(xf, w_up, ffn_conv_w, ffn_b, l, seq=seq)
        next_w = norm_mix_pre[l + 1] if l + 1 < depth else None
        h, xn = proj_norm_residual(u, w_dn, l, h, norm_ffn_post[l], next_w)
    return h.reshape(b, seq, d)
```

```python
import functools

import jax
import jax.numpy as jnp
from jax import lax
from jax.experimental import pallas as pl
from jax.experimental.pallas import tpu as pltpu

HEAD_DIM = 128
ATTN_KV_HEADS = 4
ATTN_BLOCK = 128
ROT_DIM = HEAD_DIM // 4
ROPE_THETA = 500000.0
DN_CHUNK = 64
DN_CONV = 4
FFN_CONV = 3
EPS = 1e-6
LOG2E = 1.4426950408889634

V7X_LANES = 128
V7X_SUBLANES = 8
V7X_VMEM_BYTES = 64 * 1024 * 1024

DN_PACK = 2 * V7X_LANES // DN_CHUNK
DN_HEADS_PER_STEP = 8
INV_PASSES = 1
INV_BASE = 16

F32 = jnp.float32
BF16 = jnp.bfloat16


def _dot(a, b):
    return jnp.dot(a, b, preferred_element_type=F32)


def _dot_nt(a, b):
    return lax.dot_general(a, b, (((1,), (1,)), ((), ())), preferred_element_type=F32)


def _dot_tn(a, b):
    return lax.dot_general(a, b, (((0,), (0,)), ((), ())), preferred_element_type=F32)


def _rms(x, w):
    return x * lax.rsqrt(jnp.mean(x * x, axis=-1, keepdims=True) + EPS) * w


def _silu(x):
    h = 0.5 * x
    return h + h * jnp.tanh(h)


def _vmem_limit(block_bytes):
    return int(min(V7X_VMEM_BYTES - (2 << 20), 2 * block_bytes + (12 << 20)))


def _params(semantics, block_bytes):
    return pltpu.CompilerParams(dimension_semantics=semantics,
                                vmem_limit_bytes=_vmem_limit(block_bytes))


def _tile(n, want, align=V7X_LANES):
    for t in range(min(n, want) // align * align, 0, -align):
        if n % t == 0:
            return t
    raise ValueError(f"no {align}-aligned tile divides {n}")


def _rmsnorm_kernel(x_ref, w_ref, o_ref):
    o_ref[...] = _rms(x_ref[...], w_ref[...]).astype(o_ref.dtype)


def rmsnorm(x, w):
    t, d = x.shape
    tm = _tile(t, 512)
    return pl.pallas_call(
        _rmsnorm_kernel,
        grid=(t // tm,),
        in_specs=[pl.BlockSpec((tm, d), lambda i: (i, 0)),
                  pl.BlockSpec((1, d), lambda i: (0, 0))],
        out_specs=pl.BlockSpec((tm, d), lambda i: (i, 0)),
        out_shape=jax.ShapeDtypeStruct((t, d), BF16),
        compiler_params=_params(("parallel",), tm * d * 6),
        name="rmsnorm",
    )(x, w.reshape(1, d))


def _matmul_kernel(x_ref, w_ref, o_ref):
    o_ref[...] = _dot(x_ref[...], w_ref[...]).astype(o_ref.dtype)


def matmul(x, w, layer, n):
    t, k = x.shape
    tm = _tile(t, 1024)
    tn = _tile(n, 1024)
    return pl.pallas_call(
        _matmul_kernel,
        grid=(t // tm, n // tn),
        in_specs=[pl.BlockSpec((tm, k), lambda i, j: (i, 0)),
                  pl.BlockSpec((None, k, tn), lambda i, j: (layer, 0, j))],
        out_specs=pl.BlockSpec((tm, tn), lambda i, j: (i, j)),
        out_shape=jax.ShapeDtypeStruct((t, n), BF16),
        compiler_params=_params(("parallel", "parallel"), 2 * (tm * k + k * tn + tm * tn)),
        name="in_proj",
    )(x, w)


def _rope_table_kernel(pos_ref, freq_ref, sign_ref, cos_ref, sin_ref):
    ang = pos_ref[...] * freq_ref[...]
    cos_ref[...] = jnp.cos(ang)
    sin_ref[...] = jnp.sin(ang) * sign_ref[...]


def rope_tables(positions):
    t = positions.size
    half = ROT_DIM // 2
    inv_freq = ROPE_THETA ** (-jnp.arange(0, ROT_DIM, 2, dtype=F32) / ROT_DIM)
    freq = jnp.concatenate([inv_freq, inv_freq, jnp.zeros((HEAD_DIM - ROT_DIM,), F32)]).reshape(1, HEAD_DIM)
    sign = jnp.concatenate([-jnp.ones((half,), F32), jnp.ones((HEAD_DIM - half,), F32)]).reshape(1, HEAD_DIM)
    pos = positions.astype(F32).reshape(t, 1)
    tm = _tile(t, 1024)
    row = pl.BlockSpec((1, HEAD_DIM), lambda i: (0, 0))
    out = pl.BlockSpec((tm, HEAD_DIM), lambda i: (i, 0))
    return pl.pallas_call(
        _rope_table_kernel,
        grid=(t // tm,),
        in_specs=[pl.BlockSpec((tm, 1), lambda i: (i, 0)), row, row],
        out_specs=[out, out],
        out_shape=[jax.ShapeDtypeStruct((t, HEAD_DIM), F32)] * 2,
        compiler_params=_params(("parallel",), 3 * tm * HEAD_DIM * 4),
        name="rope_tables",
    )(pos, freq, sign)


def _rotate_half_matrix():
    half = ROT_DIM // 2
    r = jnp.arange(HEAD_DIM)[:, None]
    c = jnp.arange(HEAD_DIM)[None, :]
    p = ((c < half) & (r == c + half)) | ((c >= half) & (c < ROT_DIM) & (r == c - half))
    return p.astype(BF16)


def _attn_kernel(sink_ref, q_ref, k_ref, v_ref, cos_ref, sin_ref, perm_ref, o_ref, kbuf, vbuf,
                 *, tq, n_kv, group):
    i = pl.program_id(1)
    blk = ATTN_BLOCK
    hd = HEAD_DIM

    @pl.when(i == 0)
    def _():
        kbuf[0:blk, :] = jnp.zeros((blk, n_kv * hd), kbuf.dtype)
        vbuf[0:blk, :] = jnp.zeros((blk, n_kv * hd), vbuf.dtype)

    @pl.when(i > 0)
    def _():
        kbuf[0:blk, :] = kbuf[tq:tq + blk, :]
        vbuf[0:blk, :] = vbuf[tq:tq + blk, :]

    perm = perm_ref[...]

    def rope(xb, c, s):
        return (xb.astype(F32) * c + _dot(xb, perm) * s).astype(BF16)

    cos = cos_ref[...]
    sin = sin_ref[...]
    for kh in range(n_kv):
        cols = slice(kh * hd, (kh + 1) * hd)
        kbuf[blk:blk + tq, cols] = rope(k_ref[:, cols], cos, sin)
    vbuf[blk:blk + tq, :] = v_ref[...]

    c = lax.broadcasted_iota(jnp.int32, (2 * blk, group * blk), 0)
    r = lax.broadcasted_iota(jnp.int32, (2 * blk, group * blk), 1) % blk
    band = (c > r) & (c <= r + blk)
    band_first = band & (c >= blk)
    scale = hd ** -0.5 * LOG2E
    kv_heads = range(n_kv)

    def head_ids(kh):
        return [kh * group + g for g in range(group)]

    for j in range(tq // blk):
        rows = slice(j * blk, (j + 1) * blk)
        keys = slice(j * blk, (j + 2) * blk)
        cj = cos[rows]
        sj = sin[rows]
        mask = band if j > 0 else (band_first | (band & (i > 0)))
        qs = [jnp.concatenate([rope(q_ref[rows, h * hd:(h + 1) * hd], cj, sj) for h in head_ids(kh)], axis=0)
              for kh in kv_heads]
        logits = [jnp.where(mask, _dot_nt(kbuf[keys, kh * hd:(kh + 1) * hd], qs[kh]) * scale, -jnp.inf)
                  for kh in kv_heads]
        sinks = [jnp.concatenate([jnp.full((1, blk), sink_ref[h] * LOG2E, F32) for h in head_ids(kh)], axis=1)
                 for kh in kv_heads]
        ms = [jnp.maximum(jnp.max(logits[kh], axis=0, keepdims=True), sinks[kh]) for kh in kv_heads]
        ps = [jnp.exp2(logits[kh] - ms[kh]) for kh in kv_heads]
        denoms = [jnp.sum(ps[kh], axis=0, keepdims=True) + jnp.exp2(sinks[kh] - ms[kh]) for kh in kv_heads]
        outs = [(_dot_tn(vbuf[keys, kh * hd:(kh + 1) * hd], ps[kh].astype(BF16)) / denoms[kh]).T
                for kh in kv_heads]
        for kh in kv_heads:
            for g, h in enumerate(head_ids(kh)):
                o_ref[rows, h * hd:(h + 1) * hd] = outs[kh][g * blk:(g + 1) * blk].astype(o_ref.dtype)


def attention(proj, cos, sin, sinks, *, seq, q_off, k_off, v_off, n_heads):
    b = proj.shape[0]
    n_kv = ATTN_KV_HEADS
    group = n_heads // n_kv
    qw = n_heads * HEAD_DIM
    kw = n_kv * HEAD_DIM
    tq = _tile(seq, 1024)
    assert q_off % qw == 0 and k_off % kw == 0 and v_off % kw == 0
    kern = functools.partial(_attn_kernel, tq=tq, n_kv=n_kv, group=group)
    block_bytes = tq * (2 * qw + 2 * kw) * 2 + 2 * tq * HEAD_DIM * 4 + 2 * (tq + ATTN_BLOCK) * kw * 2
    return pl.pallas_call(
        kern,
        grid=(b, seq // tq),
        in_specs=[pl.BlockSpec(memory_space=pltpu.SMEM),
                  pl.BlockSpec((None, tq, qw), lambda bi, i: (bi, i, q_off // qw)),
                  pl.BlockSpec((None, tq, kw), lambda bi, i: (bi, i, k_off // kw)),
                  pl.BlockSpec((None, tq, kw), lambda bi, i: (bi, i, v_off // kw)),
                  pl.BlockSpec((None, tq, HEAD_DIM), lambda bi, i: (bi, i, 0)),
                  pl.BlockSpec((None, tq, HEAD_DIM), lambda bi, i: (bi, i, 0)),
                  pl.BlockSpec((HEAD_DIM, HEAD_DIM), lambda bi, i: (0, 0))],
        out_specs=pl.BlockSpec((None, tq, qw), lambda bi, i: (bi, i, 0)),
        out_shape=jax.ShapeDtypeStruct((b, seq, qw), BF16),
        scratch_shapes=[pltpu.VMEM((tq + ATTN_BLOCK, kw), BF16),
                        pltpu.VMEM((tq + ATTN_BLOCK, kw), BF16)],
        compiler_params=_params(("parallel", "arbitrary"), block_bytes),
        name="swa_attention",
    )(sinks, proj, proj, proj, cos, sin, _rotate_half_matrix())


GATE_GROUPS = 5


def _gate_lanes(b_part, a_part, hg):
    n_heads = a_part.shape[-1]
    assert n_heads % hg == 0 and GATE_GROUPS * hg <= V7X_LANES
    pad = jnp.zeros(a_part.shape[:-1] + (V7X_LANES - GATE_GROUPS * hg,), a_part.dtype)
    parts = []
    for g in range(n_heads // hg):
        heads = slice(g * hg, (g + 1) * hg)
        parts += [b_part[..., heads]] + [a_part[..., heads]] * (GATE_GROUPS - 1) + [pad]
    return jnp.concatenate(parts, axis=-1)


def _gates_kernel(x_ref, w_ref, alog_ref, dt_ref, gt_ref, gr_ref, *, hg, chunk):
    tm = x_ref.shape[0]
    pw = DN_PACK * chunk
    val = _dot(x_ref[...], w_ref[...])
    quantity = (lax.broadcasted_iota(jnp.int32, val.shape, 1) % V7X_LANES) // hg
    beta = jax.nn.sigmoid(val)
    z = val + dt_ref[...]
    softplus = jnp.maximum(z, 0.0) + jnp.log1p(jnp.exp(-jnp.abs(z)))
    g = jnp.where((quantity >= 1) & (quantity < GATE_GROUPS), -jnp.exp(alog_ref[...]) * softplus, 0.0)
    r = lax.broadcasted_iota(jnp.int32, (tm, tm), 0)
    c = lax.broadcasted_iota(jnp.int32, (tm, tm), 1)
    tri = (((r // chunk) == (c // chunk)) & (c <= r)).astype(BF16)
    hi = g.astype(BF16)
    rem = g - hi.astype(F32)
    mid = rem.astype(BF16)
    lo = (rem - mid.astype(F32)).astype(BF16)
    gc = _dot(tri, hi) + _dot(tri, mid) + _dot(tri, lo)
    total = jnp.concatenate(
        [jnp.broadcast_to(gc[(ci + 1) * chunk - 1:(ci + 1) * chunk, :], (chunk, gc.shape[1]))
         for ci in range(tm // chunk)], axis=0)
    gt_ref[...] = jnp.where(quantity == 0, beta,
                  jnp.where(quantity == 1, gc,
                  jnp.where(quantity == 2, jnp.exp(gc),
                  jnp.where(quantity == 3, jnp.exp(total - gc),
                  jnp.where(quantity == 4, jnp.exp(total), 0.0)))))
    gct = gc.T
    for pi in range(tm // pw):
        for grp in range(gc.shape[1] // V7X_LANES):
            lo_row = grp * V7X_LANES + hg
            gr_ref[pi, grp * hg:(grp + 1) * hg, :] = gct[lo_row:lo_row + hg, pi * pw:(pi + 1) * pw]


def dn_gates(xn, w_gate, layer, a_log, dt_bias, *, hg):
    t, d = xn.shape
    n_heads = a_log.shape[0]
    gl = w_gate.shape[2]
    pw = DN_PACK * DN_CHUNK
    tm = _tile(t, 512, align=pw)
    alog = _gate_lanes(jnp.zeros_like(a_log), a_log, hg).reshape(1, gl)
    dt = _gate_lanes(jnp.zeros_like(dt_bias), dt_bias, hg).reshape(1, gl)
    row = pl.BlockSpec((1, gl), lambda i: (0, 0))
    kern = functools.partial(_gates_kernel, hg=hg, chunk=DN_CHUNK)
    return pl.pallas_call(
        kern,
        grid=(t // tm,),
        in_specs=[pl.BlockSpec((tm, d), lambda i: (i, 0)),
                  pl.BlockSpec((None, d, gl), lambda i: (layer, 0, 0)), row, row],
        out_specs=[pl.BlockSpec((tm, gl), lambda i: (i, 0)),
                   pl.BlockSpec((tm // pw, n_heads, pw), lambda i: (i, 0, 0))],
        out_shape=[jax.ShapeDtypeStruct((t, gl), F32),
                   jax.ShapeDtypeStruct((t // pw, n_heads, pw), F32)],
        compiler_params=_params(("parallel",), tm * d * 2 + d * gl * 2 + 4 * tm * gl * 4 + 4 * tm * tm),
        name="dn_gates",
    )(xn, w_gate, alog, dt)


def _block_diag(xp):
    n, m = xp.shape
    tiled = jnp.concatenate([xp] * (m // n), axis=0)
    r = lax.broadcasted_iota(jnp.int32, (m, m), 0) // n
    c = lax.broadcasted_iota(jnp.int32, (m, m), 1) // n
    return jnp.where(r == c, tiled, jnp.zeros_like(tiled))


def _packed_dot(xp, yp, passes):
    xh = xp.astype(BF16)
    yh = yp.astype(BF16)
    out = _dot(xh, _block_diag(yh))
    if passes >= 3:
        xl = (xp - xh.astype(F32)).astype(BF16)
        yl = (yp - yh.astype(F32)).astype(BF16)
        out = out + _dot(xh, _block_diag(yl)) + _dot(xl, _block_diag(yh))
    return out


def _packed_unit_lower_inverses(aps, passes):
    n = aps[0].shape[0]
    r = lax.broadcasted_iota(jnp.int32, aps[0].shape, 0)
    c = lax.broadcasted_iota(jnp.int32, aps[0].shape, 1) % n
    eye = (r == c).astype(F32)
    size = INV_BASE
    in_blk = (r // size) == (c // size)
    xs = [jnp.where(in_blk, -ap, 0.0) for ap in aps]
    invs = [eye + x for x in xs]
    power = 1
    while 2 * power < size:
        xs = [_packed_dot(x, x, passes) for x in xs]
        invs = [inv + _packed_dot(inv, x, passes) for inv, x in zip(invs, xs)]
        power *= 2
    while size < n:
        merged = (r // (2 * size)) == (c // (2 * size))
        offs = [_packed_dot(jnp.where(merged & ~in_blk, ap, 0.0), inv, passes) for ap, inv in zip(aps, invs)]
        invs = [inv - _packed_dot(inv, off, passes) for inv, off in zip(invs, offs)]
        in_blk = merged
        size *= 2
    return invs


def _dn_kernel(q_ref, k_ref, v_ref, z_ref, wq_ref, wk_ref, wv_ref, gt_ref, gr_ref, nw_ref, o_ref,
               qtail, ktail, vtail, qn, kn, vn, rhs_s, u_s, wq_s, kd_s, qkm_s, vnew_s, qs_s, state,
               *, ts, hg, chunk):
    s = pl.program_id(2)
    gi = pl.program_id(1)
    hd = HEAD_DIM
    halo = V7X_SUBLANES

    @pl.when(s == 0)
    def _():
        state[...] = jnp.zeros(state.shape, F32)
        for tail in (qtail, ktail, vtail):
            tail[...] = jnp.zeros(tail.shape, F32)

    def conv_silu(src_ref, tail, w_ref):
        x = src_ref[...].astype(F32)
        xe = jnp.concatenate([tail[...], x], axis=0)
        tail[...] = x[ts - halo:, :]
        return _silu(_causal_conv(xe, w_ref[...])[halo:])

    def l2norm(x, scale):
        return x * (lax.rsqrt(jnp.sum(x * x, axis=-1, keepdims=True) + EPS) * scale)

    qc = conv_silu(q_ref, qtail, wq_ref)
    kc = conv_silu(k_ref, ktail, wk_ref)
    vn[...] = conv_silu(v_ref, vtail, wv_ref)
    for hh in range(hg):
        cols = slice(hh * hd, (hh + 1) * hd)
        qn[:, cols] = l2norm(qc[:, cols], hd ** -0.5)
        kn[:, cols] = l2norm(kc[:, cols], 1.0)

    pack = DN_PACK
    pw = pack * chunk
    n_pack = ts // pw
    r = lax.broadcasted_iota(jnp.int32, (pw, pw), 0)
    c = lax.broadcasted_iota(jnp.int32, (pw, pw), 1)
    same = (r // chunk) == (c // chunk)
    tri = same & (r >= c)
    strict = same & (r > c)
    lane = lax.broadcasted_iota(jnp.int32, (pw, V7X_LANES), 1)
    nw = nw_ref[...]

    jobs = [(hh, p) for hh in range(hg) for p in range(n_pack)]
    chunk_decay = {}
    aps = []
    for hh, p in jobs:
        cols = slice(hh * hd, (hh + 1) * hd)
        rows = slice(p * pw, (p + 1) * pw)
        gt = gt_ref[rows, :]
        beta, gc, eg, erem, etot = (jnp.sum(jnp.where(lane == qi * hg + hh, gt, 0.0), axis=-1, keepdims=True)
                                    for qi in range(GATE_GROUPS))
        gc_row = gr_ref[p, pl.ds(gi * hg + hh, 1), :]
        q = qn[rows, cols]
        k = kn[rows, cols]
        v = vn[rows, cols]
        kb = k.astype(BF16)
        sims = _dot_nt(jnp.concatenate([kb, q.astype(BF16)], axis=0), kb)
        decay = jnp.exp(jnp.where(tri, gc - gc_row, -jnp.inf))
        a_full = jnp.where(strict, beta * sims[0:pw] * decay, 0.0)
        ap = a_full[0:chunk]
        for bb in range(1, pack):
            ap = ap + a_full[bb * chunk:(bb + 1) * chunk]
        aps.append(ap)
        rhs_s[hh, rows, :] = jnp.concatenate([v * beta, k * (beta * eg)], axis=1).astype(BF16)
        qg = (q * eg).astype(BF16)
        for bb in range(pack):
            wq_s[hh, p * pack + bb, chunk:2 * chunk, :] = qg[bb * chunk:(bb + 1) * chunk]
            chunk_decay[(hh, p * pack + bb)] = etot[bb * chunk:bb * chunk + 1, :]
        kd_s[hh, rows, :] = (k * erem).astype(BF16)
        qkm_s[hh, p] = (sims[pw:2 * pw] * decay).astype(BF16)

    tps = _packed_unit_lower_inverses(aps, INV_PASSES)

    for (hh, p), tp in zip(jobs, tps):
        rows = slice(p * pw, (p + 1) * pw)
        uw = _dot(_block_diag(tp.astype(BF16)), rhs_s[hh, rows, :])
        u_s[hh, rows, :] = uw[:, 0:hd]
        w = uw[:, hd:2 * hd].astype(BF16)
        for bb in range(pack):
            wq_s[hh, p * pack + bb, 0:chunk, :] = w[bb * chunk:(bb + 1) * chunk]

    for cidx in range(ts // chunk):
        rows = slice(cidx * chunk, (cidx + 1) * chunk)
        for hh in range(hg):
            st = state[hh]
            ws_qs = _dot(wq_s[hh, cidx], st.astype(BF16))
            v_new = (u_s[hh, rows, :] - ws_qs[0:chunk]).astype(BF16)
            vnew_s[hh, rows, :] = v_new
            qs_s[hh, rows, :] = ws_qs[chunk:2 * chunk]
            state[hh] = st * chunk_decay[(hh, cidx)] + _dot_tn(kd_s[hh, rows, :], v_new)

    for hh in range(hg):
        cols = slice(hh * hd, (hh + 1) * hd)
        for p in range(n_pack):
            rows = slice(p * pw, (p + 1) * pw)
            o = qs_s[hh, rows, :] + _dot(qkm_s[hh, p], vnew_s[hh, rows, :])
            zz = z_ref[rows, cols].astype(F32)
            o = o * lax.rsqrt(jnp.mean(o * o, axis=-1, keepdims=True) + EPS) * nw * _silu(zz)
            o_ref[rows, cols] = o.astype(o_ref.dtype)


def delta_net(proj, proj_z, gt, gr, conv_w, layer, norm_w, *, seq, q_off, k_off, v_off, z_off, n_heads):
    b = proj.shape[0]
    hd = HEAD_DIM
    hg = min(DN_HEADS_PER_STEP, n_heads)
    pw = DN_PACK * DN_CHUNK
    ts = _tile(seq, 512, align=pw)
    gw = hg * hd
    kw = n_heads * hd
    assert n_heads % hg == 0
    for off in (q_off, k_off, v_off, z_off, kw):
        assert off % gw == 0
    kern = functools.partial(_dn_kernel, ts=ts, hg=hg, chunk=DN_CHUNK)

    def col_spec(off):
        return pl.BlockSpec((None, ts, gw), lambda bi, g, s: (bi, s, off // gw + g))

    def w_spec(off):
        return pl.BlockSpec((None, DN_CONV, gw), lambda bi, g, s: (layer, 0, off // gw + g))

    scratch = ([pltpu.VMEM((V7X_SUBLANES, gw), F32)] * 3
               + [pltpu.VMEM((ts, gw), F32)] * 3
               + [pltpu.VMEM((hg, ts, 2 * hd), BF16),
                  pltpu.VMEM((hg, ts, hd), F32),
                  pltpu.VMEM((hg, ts // DN_CHUNK, 2 * DN_CHUNK, hd), BF16),
                  pltpu.VMEM((hg, ts, hd), BF16),
                  pltpu.VMEM((hg, ts // pw, pw, pw), BF16),
                  pltpu.VMEM((hg, ts, hd), BF16),
                  pltpu.VMEM((hg, ts, hd), F32),
                  pltpu.VMEM((hg, hd, hd), F32)])
    block_bytes = (5 * ts * gw * 2 + ts * V7X_LANES * 4 + (ts // pw) * n_heads * pw * 4
                   + 3 * (ts + V7X_SUBLANES) * gw * 4 + 3 * ts * gw * 4 + hg * ts * hd * 18)
    return pl.pallas_call(
        kern,
        grid=(b, n_heads // hg, seq // ts),
        in_specs=[col_spec(q_off), col_spec(k_off), col_spec(v_off), col_spec(z_off),
                  w_spec(0), w_spec(kw), w_spec(2 * kw),
                  pl.BlockSpec((None, ts, V7X_LANES), lambda bi, g, s: (bi, s, g)),
                  pl.BlockSpec((None, ts // pw, n_heads, pw), lambda bi, g, s: (bi, s, 0, 0)),
                  pl.BlockSpec((1, hd), lambda bi, g, s: (0, 0))],
        out_specs=pl.BlockSpec((None, ts, gw), lambda bi, g, s: (bi, s, g)),
        out_shape=jax.ShapeDtypeStruct((b, seq, kw), BF16),
        scratch_shapes=scratch,
        compiler_params=_params(("parallel", "parallel", "arbitrary"), block_bytes),
        name="delta_net",
    )(proj, proj, proj, proj_z, conv_w, conv_w, conv_w, gt, gr, norm_w.reshape(1, hd))


def _merge_kernel(oa_ref, od_ref, ga_ref, gd_ref, wa_ref, wd_ref, y_ref):
    a = _dot(oa_ref[...], wa_ref[...])
    sa = jax.nn.sigmoid(ga_ref[...].astype(F32))
    sd = jax.nn.sigmoid(gd_ref[...].astype(F32))
    d = _dot(od_ref[...], wd_ref[...])
    y_ref[...] = (sa * a + sd * d).astype(y_ref.dtype)


def branch_merge(o_a, o_d, proj, w_a, w_d, layer, *, ga_off, gd_off):
    t, ka = o_a.shape
    kd = o_d.shape[1]
    n = w_a.shape[2]
    tm = _tile(t, 1024)
    tn = _tile(n, 1024)
    assert ga_off % tn == 0 and gd_off % tn == 0
    block_bytes = 2 * (tm * ka + tm * kd + 3 * tm * tn + ka * tn + kd * tn)
    return pl.pallas_call(
        _merge_kernel,
        grid=(t // tm, n // tn),
        in_specs=[pl.BlockSpec((tm, ka), lambda i, j: (i, 0)),
                  pl.BlockSpec((tm, kd), lambda i, j: (i, 0)),
                  pl.BlockSpec((tm, tn), lambda i, j: (i, ga_off // tn + j)),
                  pl.BlockSpec((tm, tn), lambda i, j: (i, gd_off // tn + j)),
                  pl.BlockSpec((None, ka, tn), lambda i, j: (layer, 0, j)),
                  pl.BlockSpec((None, kd, tn), lambda i, j: (layer, 0, j))],
        out_specs=pl.BlockSpec((tm, tn), lambda i, j: (i, j)),
        out_shape=jax.ShapeDtypeStruct((t, n), BF16),
        compiler_params=_params(("parallel", "parallel"), block_bytes),
        name="branch_merge",
    )(o_a, o_d, proj, proj, w_a, w_d)


def _proj_norm_res_kernel(*refs, with_next):
    if with_next:
        x_ref, w_ref, h_ref, post_ref, next_ref, h_out, xn_out = refs
    else:
        x_ref, w_ref, h_ref, post_ref, h_out = refs
    hn = h_ref[...] + _rms(_dot(x_ref[...], w_ref[...]), post_ref[...])
    h_out[...] = hn
    if with_next:
        xn_out[...] = _rms(hn, next_ref[...]).astype(xn_out.dtype)


def proj_norm_residual(x, w, layer, h, post_w, next_w):
    t, k = x.shape
    d = w.shape[2]
    tm = _tile(t, 256)
    with_next = next_w is not None
    kern = functools.partial(_proj_norm_res_kernel, with_next=with_next)
    row = pl.BlockSpec((1, d), lambda i: (0, 0))
    tile = pl.BlockSpec((tm, d), lambda i: (i, 0))
    in_specs = [pl.BlockSpec((tm, k), lambda i: (i, 0)),
                pl.BlockSpec((None, k, d), lambda i: (layer, 0, 0), pipeline_mode=pl.Buffered(1)), tile, row]
    args = [x, w, h, post_w.reshape(1, d)]
    out_specs = [tile]
    out_shape = [jax.ShapeDtypeStruct((t, d), F32)]
    if with_next:
        in_specs.append(row)
        args.append(next_w.reshape(1, d))
        out_specs.append(tile)
        out_shape.append(jax.ShapeDtypeStruct((t, d), BF16))
    block_bytes = tm * k * 2 + tm * d * (4 + 4 + 2) + (k * d * 2) // 2 + tm * d * 4
    res = pl.pallas_call(
        kern,
        grid=(t // tm,),
        in_specs=in_specs,
        out_specs=out_specs,
        out_shape=out_shape,
        compiler_params=pltpu.CompilerParams(
            dimension_semantics=("parallel",),
            vmem_limit_bytes=int(min(V7X_VMEM_BYTES - (2 << 20), 2 * block_bytes + (4 << 20)))),
        name="proj_norm_residual",
    )(*args)
    return (res[0], res[1]) if with_next else (res[0], None)


def _causal_conv(u, taps):
    y = u * taps[0:1, :]
    for j in range(1, taps.shape[0]):
        y = pltpu.roll(y, 1, 0) + u * taps[j:j + 1, :]
    return y


def _ffn_up_kernel(x_ref, wg_ref, wv_ref, cg_ref, cv_ref, bg_ref, bv_ref, o_ref, tail_g, tail_v,
                   *, tiles_per_seq, sub):
    i = pl.program_id(1)
    halo = V7X_SUBLANES
    tm, tn = o_ref.shape
    x = x_ref[...]
    blocks = [slice(lo, min(lo + sub, tn)) for lo in range(0, tn, sub)]

    @pl.when(i == 0)
    def _():
        tail_g[...] = jnp.zeros(tail_g.shape, F32)
        tail_v[...] = jnp.zeros(tail_v.shape, F32)

    def project(cols):
        return _dot(x, wg_ref[:, cols]), _dot(x, wv_ref[:, cols])

    def conv(u, tail, cols, c_ref, b_ref):
        hist = jnp.where((i % tiles_per_seq) != 0, tail[:, cols], 0.0)
        tail[:, cols] = u[tm - halo:, :]
        return _causal_conv(jnp.concatenate([hist, u], axis=0), c_ref[:, cols])[halo:] + b_ref[:, cols]

    pending = project(blocks[0])
    for s, cols in enumerate(blocks):
        ug, uv = pending
        if s + 1 < len(blocks):
            pending = project(blocks[s + 1])
        gate = conv(ug, tail_g, cols, cg_ref, bg_ref)
        val = conv(uv, tail_v, cols, cv_ref, bv_ref)
        o_ref[:, cols] = (_silu(gate) * val).astype(o_ref.dtype)


def ffn_up(xn, w_up, conv_w, conv_b, layer, *, seq):
    t, d = xn.shape
    dff = w_up.shape[2] // 2
    tm = _tile(seq, 1024)
    sub = 8 * V7X_LANES
    tn = _tile(dff, 2816)
    halo = V7X_SUBLANES
    nj = dff // tn
    kern = functools.partial(_ffn_up_kernel, tiles_per_seq=seq // tm, sub=sub)

    def w_spec(off):
        return pl.BlockSpec((None, d, tn), lambda j, i: (layer, 0, off + j), pipeline_mode=pl.Buffered(1))

    block_bytes = tm * d * 2 + tm * tn * 2 + d * tn * 2 + 2 * (tm + halo) * sub * 4
    return pl.pallas_call(
        kern,
        grid=(nj, t // tm),
        in_specs=[pl.BlockSpec((tm, d), lambda j, i: (i, 0)),
                  w_spec(0), w_spec(nj),
                  pl.BlockSpec((None, FFN_CONV, tn), lambda j, i: (layer, 0, j)),
                  pl.BlockSpec((None, FFN_CONV, tn), lambda j, i: (layer, 0, nj + j)),
                  pl.BlockSpec((None, 1, tn), lambda j, i: (layer, 0, j)),
                  pl.BlockSpec((None, 1, tn), lambda j, i: (layer, 0, nj + j))],
        out_specs=pl.BlockSpec((tm, tn), lambda j, i: (i, j)),
        out_shape=jax.ShapeDtypeStruct((t, dff), BF16),
        scratch_shapes=[pltpu.VMEM((halo, tn), F32)] * 2,
        compiler_params=_params(("arbitrary", "arbitrary"), block_bytes),
        name="ffn_up",
    )(xn, w_up, w_up, conv_w, conv_w, conv_b, conv_b)


def kernel(x, positions, norm_mix_pre, w_in, attn_sinks, dn_conv_w, dn_a_log, dn_dt_bias, dn_norm_w,
           w_branch_attn, w_branch_dn, w_out, norm_mix_post, norm_ffn_pre, w_ffn_up, ffn_conv_w,
           ffn_conv_b, w_ffn_down, norm_ffn_post):
    b, seq, d = x.shape
    depth = w_in.shape[0]
    t = b * seq
    n_attn = attn_sinks.shape[1]
    n_dn = dn_a_log.shape[1]
    qa_w = n_attn * HEAD_DIM
    kv_w = ATTN_KV_HEADS * HEAD_DIM
    dn_w = n_dn * HEAD_DIM

    edges = [0]
    for wdt in (qa_w, kv_w, kv_w, 3 * dn_w, n_dn, n_dn, dn_w, d, d):
        edges.append(edges[-1] + wdt)
    head_w = edges[4]
    tail_w = edges[9] - edges[6]
    w_head = w_in[:, :, :head_w].astype(BF16)
    w_tail = w_in[:, :, edges[6]:edges[9]].astype(BF16)
    dn_hg = min(DN_HEADS_PER_STEP, n_dn)
    w_gate = _gate_lanes(w_in[:, :, edges[4]:edges[5]], w_in[:, :, edges[5]:edges[6]], dn_hg).astype(BF16)
    q_off, k_off, v_off = 0, qa_w, qa_w + kv_w
    dq_off = qa_w + 2 * kv_w
    dk_off, dv_off = dq_off + dn_w, dq_off + 2 * dn_w
    z_off, ga_off, gd_off = 0, dn_w, dn_w + d

    w_ba = w_branch_attn.astype(BF16)
    w_bd = w_branch_dn.astype(BF16)
    w_o = w_out.astype(BF16)
    w_up = w_ffn_up.astype(BF16)
    w_dn = w_ffn_down.astype(BF16)
    ffn_b = ffn_conv_b.reshape(depth, 1, ffn_conv_b.shape[1])

    cos, sin = rope_tables(positions)
    cos = cos.reshape(b, seq, HEAD_DIM)
    sin = sin.reshape(b, seq, HEAD_DIM)

    h = x.reshape(t, d)
    xn = rmsnorm(h, norm_mix_pre[0])
    for l in range(depth):
        proj_head = matmul(xn, w_head, l, head_w)
        proj_tail = matmul(xn, w_tail, l, tail_w)
        head3 = proj_head.reshape(b, seq, head_w)
        o_a = attention(head3, cos, sin, attn_sinks[l], seq=seq, q_off=q_off, k_off=k_off,
                        v_off=v_off, n_heads=n_attn)
        gt, gr = dn_gates(xn, w_gate, l, dn_a_log[l], dn_dt_bias[l], hg=dn_hg)
        o_d = delta_net(head3, proj_tail.reshape(b, seq, tail_w), gt.reshape(b, seq, gt.shape[1]),
                        gr.reshape(b, seq // (DN_PACK * DN_CHUNK), n_dn, DN_PACK * DN_CHUNK),
                        dn_conv_w, l, dn_norm_w[l],
                        seq=seq, q_off=dq_off, k_off=dk_off, v_off=dv_off, z_off=z_off, n_heads=n_dn)
        y = branch_merge(o_a.reshape(t, qa_w), o_d.reshape(t, dn_w), proj_tail, w_ba, w_bd, l,
                         ga_off=ga_off, gd_off=gd_off)
        h, xf = proj_norm_residual(y, w_o, l, h, norm_mix_post[l], norm_ffn_pre[l])
        u = ffn_up(xf, w_up, ffn_conv_w, ffn_b, l, seq=seq)
        next_w = norm_mix_pre[l + 1] if l + 1 < depth else None
        h, xn = proj_norm_residual(u, w_dn, l, h, norm_ffn_post[l], next_w)
    return h.reshape(b, seq, d)
```

```python
import functools

import jax
import jax.numpy as jnp
from jax import lax
from jax.experimental import pallas as pl
from jax.experimental.pallas import tpu as pltpu

HEAD_DIM = 128
ATTN_KV_HEADS = 4
ATTN_BLOCK = 128
ROT_DIM = HEAD_DIM // 4
ROPE_THETA = 500000.0
DN_CHUNK = 64
DN_CONV = 4
FFN_CONV = 3
EPS = 1e-6
LOG2E = 1.4426950408889634

V7X_LANES = 128
V7X_SUBLANES = 8
V7X_VMEM_BYTES = 64 * 1024 * 1024

DN_PACK = 2 * V7X_LANES // DN_CHUNK
DN_HEADS_PER_STEP = 8
INV_PASSES = 1
INV_BASE = 16

F32 = jnp.float32
BF16 = jnp.bfloat16


def _dot(a, b):
    return jnp.dot(a, b, preferred_element_type=F32)


def _dot_nt(a, b):
    return lax.dot_general(a, b, (((1,), (1,)), ((), ())), preferred_element_type=F32)


def _dot_tn(a, b):
    return lax.dot_general(a, b, (((0,), (0,)), ((), ())), preferred_element_type=F32)


def _rms(x, w):
    return x * lax.rsqrt(jnp.mean(x * x, axis=-1, keepdims=True) + EPS) * w


def _silu(x):
    h = 0.5 * x
    return h + h * jnp.tanh(h)


def _vmem_limit(block_bytes):
    return int(min(V7X_VMEM_BYTES - (2 << 20), 2 * block_bytes + (12 << 20)))


def _params(semantics, block_bytes):
    return pltpu.CompilerParams(dimension_semantics=semantics,
                                vmem_limit_bytes=_vmem_limit(block_bytes))


def _tile(n, want, align=V7X_LANES):
    for t in range(min(n, want) // align * align, 0, -align):
        if n % t == 0:
            return t
    raise ValueError(f"no {align}-aligned tile divides {n}")


def _rmsnorm_kernel(x_ref, w_ref, o_ref):
    o_ref[...] = _rms(x_ref[...], w_ref[...]).astype(o_ref.dtype)


def rmsnorm(x, w):
    t, d = x.shape
    tm = _tile(t, 512)
    return pl.pallas_call(
        _rmsnorm_kernel,
        grid=(t // tm,),
        in_specs=[pl.BlockSpec((tm, d), lambda i: (i, 0)),
                  pl.BlockSpec((1, d), lambda i: (0, 0))],
        out_specs=pl.BlockSpec((tm, d), lambda i: (i, 0)),
        out_shape=jax.ShapeDtypeStruct((t, d), BF16),
        compiler_params=_params(("parallel",), tm * d * 6),
        name="rmsnorm",
    )(x, w.reshape(1, d))


def _matmul_kernel(x_ref, w_ref, o_ref):
    o_ref[...] = _dot(x_ref[...], w_ref[...]).astype(o_ref.dtype)


def matmul(x, w, layer, n):
    t, k = x.shape
    tm = _tile(t, 1024)
    tn = _tile(n, 3072)
    return pl.pallas_call(
        _matmul_kernel,
        grid=(t // tm, n // tn),
        in_specs=[pl.BlockSpec((tm, k), lambda i, j: (i, 0)),
                  pl.BlockSpec((None, k, tn), lambda i, j: (layer, 0, j))],
        out_specs=pl.BlockSpec((tm, tn), lambda i, j: (i, j)),
        out_shape=jax.ShapeDtypeStruct((t, n), BF16),
        compiler_params=_params(("parallel", "parallel"), 2 * (tm * k + k * tn + tm * tn)),
        name="in_proj",
    )(x, w)


def _rope_table_kernel(pos_ref, freq_ref, sign_ref, cos_ref, sin_ref):
    ang = pos_ref[...] * freq_ref[...]
    cos_ref[...] = jnp.cos(ang)
    sin_ref[...] = jnp.sin(ang) * sign_ref[...]


def rope_tables(positions):
    t = positions.size
    half = ROT_DIM // 2
    inv_freq = ROPE_THETA ** (-jnp.arange(0, ROT_DIM, 2, dtype=F32) / ROT_DIM)
    freq = jnp.concatenate([inv_freq, inv_freq, jnp.zeros((HEAD_DIM - ROT_DIM,), F32)]).reshape(1, HEAD_DIM)
    sign = jnp.concatenate([-jnp.ones((half,), F32), jnp.ones((HEAD_DIM - half,), F32)]).reshape(1, HEAD_DIM)
    pos = positions.astype(F32).reshape(t, 1)
    tm = _tile(t, 1024)
    row = pl.BlockSpec((1, HEAD_DIM), lambda i: (0, 0))
    out = pl.BlockSpec((tm, HEAD_DIM), lambda i: (i, 0))
    return pl.pallas_call(
        _rope_table_kernel,
        grid=(t // tm,),
        in_specs=[pl.BlockSpec((tm, 1), lambda i: (i, 0)), row, row],
        out_specs=[out, out],
        out_shape=[jax.ShapeDtypeStruct((t, HEAD_DIM), F32)] * 2,
        compiler_params=_params(("parallel",), 3 * tm * HEAD_DIM * 4),
        name="rope_tables",
    )(pos, freq, sign)


def _rotate_half_matrix():
    half = ROT_DIM // 2
    r = jnp.arange(HEAD_DIM)[:, None]
    c = jnp.arange(HEAD_DIM)[None, :]
    p = ((c < half) & (r == c + half)) | ((c >= half) & (c < ROT_DIM) & (r == c - half))
    return p.astype(BF16)


def _attn_kernel(sink_ref, q_ref, k_ref, v_ref, cos_ref, sin_ref, perm_ref, o_ref, kbuf, vbuf,
                 *, tq, n_kv, group):
    i = pl.program_id(1)
    blk = ATTN_BLOCK
    hd = HEAD_DIM

    @pl.when(i == 0)
    def _():
        kbuf[0:blk, :] = jnp.zeros((blk, n_kv * hd), kbuf.dtype)
        vbuf[0:blk, :] = jnp.zeros((blk, n_kv * hd), vbuf.dtype)

    @pl.when(i > 0)
    def _():
        kbuf[0:blk, :] = kbuf[tq:tq + blk, :]
        vbuf[0:blk, :] = vbuf[tq:tq + blk, :]

    perm = perm_ref[...]

    def rope(xb, c, s):
        return (xb.astype(F32) * c + _dot(xb, perm) * s).astype(BF16)

    cos = cos_ref[...]
    sin = sin_ref[...]
    for kh in range(n_kv):
        cols = slice(kh * hd, (kh + 1) * hd)
        kbuf[blk:blk + tq, cols] = rope(k_ref[:, cols], cos, sin)
    vbuf[blk:blk + tq, :] = v_ref[...]

    c = lax.broadcasted_iota(jnp.int32, (2 * blk, group * blk), 0)
    r = lax.broadcasted_iota(jnp.int32, (2 * blk, group * blk), 1) % blk
    band = (c > r) & (c <= r + blk)
    band_first = band & (c >= blk)
    scale = hd ** -0.5 * LOG2E
    kv_heads = range(n_kv)

    def head_ids(kh):
        return [kh * group + g for g in range(group)]

    for j in range(tq // blk):
        rows = slice(j * blk, (j + 1) * blk)
        keys = slice(j * blk, (j + 2) * blk)
        cj = cos[rows]
        sj = sin[rows]
        mask = band if j > 0 else (band_first | (band & (i > 0)))
        qs = [jnp.concatenate([rope(q_ref[rows, h * hd:(h + 1) * hd], cj, sj) for h in head_ids(kh)], axis=0)
              for kh in kv_heads]
        logits = [jnp.where(mask, _dot_nt(kbuf[keys, kh * hd:(kh + 1) * hd], qs[kh]) * scale, -jnp.inf)
                  for kh in kv_heads]
        sinks = [jnp.concatenate([jnp.full((1, blk), sink_ref[h] * LOG2E, F32) for h in head_ids(kh)], axis=1)
                 for kh in kv_heads]
        ms = [jnp.maximum(jnp.max(logits[kh], axis=0, keepdims=True), sinks[kh]) for kh in kv_heads]
        ps = [jnp.exp2(logits[kh] - ms[kh]) for kh in kv_heads]
        denoms = [jnp.sum(ps[kh], axis=0, keepdims=True) + jnp.exp2(sinks[kh] - ms[kh]) for kh in kv_heads]
        outs = [(_dot_tn(vbuf[keys, kh * hd:(kh + 1) * hd], ps[kh].astype(BF16)) / denoms[kh]).T
                for kh in kv_heads]
        for kh in kv_heads:
            for g, h in enumerate(head_ids(kh)):
                o_ref[rows, h * hd:(h + 1) * hd] = outs[kh][g * blk:(g + 1) * blk].astype(o_ref.dtype)


def attention(proj, cos, sin, sinks, *, seq, q_off, k_off, v_off, n_heads):
    b = proj.shape[0]
    n_kv = ATTN_KV_HEADS
    group = n_heads // n_kv
    qw = n_heads * HEAD_DIM
    kw = n_kv * HEAD_DIM
    tq = _tile(seq, 1024)
    assert q_off % qw == 0 and k_off % kw == 0 and v_off % kw == 0
    kern = functools.partial(_attn_kernel, tq=tq, n_kv=n_kv, group=group)
    block_bytes = tq * (2 * qw + 2 * kw) * 2 + 2 * tq * HEAD_DIM * 4 + 2 * (tq + ATTN_BLOCK) * kw * 2
    return pl.pallas_call(
        kern,
        grid=(b, seq // tq),
        in_specs=[pl.BlockSpec(memory_space=pltpu.SMEM),
                  pl.BlockSpec((None, tq, qw), lambda bi, i: (bi, i, q_off // qw)),
                  pl.BlockSpec((None, tq, kw), lambda bi, i: (bi, i, k_off // kw)),
                  pl.BlockSpec((None, tq, kw), lambda bi, i: (bi, i, v_off // kw)),
                  pl.BlockSpec((None, tq, HEAD_DIM), lambda bi, i: (bi, i, 0)),
                  pl.BlockSpec((None, tq, HEAD_DIM), lambda bi, i: (bi, i, 0)),
                  pl.BlockSpec((HEAD_DIM, HEAD_DIM), lambda bi, i: (0, 0))],
        out_specs=pl.BlockSpec((None, tq, qw), lambda bi, i: (bi, i, 0)),
        out_shape=jax.ShapeDtypeStruct((b, seq, qw), BF16),
        scratch_shapes=[pltpu.VMEM((tq + ATTN_BLOCK, kw), BF16),
                        pltpu.VMEM((tq + ATTN_BLOCK, kw), BF16)],
        compiler_params=_params(("parallel", "arbitrary"), block_bytes),
        name="swa_attention",
    )(sinks, proj, proj, proj, cos, sin, _rotate_half_matrix())


GATE_GROUPS = 5


def _gate_lanes(b_part, a_part, hg):
    n_heads = a_part.shape[-1]
    assert n_heads % hg == 0 and GATE_GROUPS * hg <= V7X_LANES
    pad = jnp.zeros(a_part.shape[:-1] + (V7X_LANES - GATE_GROUPS * hg,), a_part.dtype)
    parts = []
    for g in range(n_heads // hg):
        heads = slice(g * hg, (g + 1) * hg)
        parts += [b_part[..., heads]] + [a_part[..., heads]] * (GATE_GROUPS - 1) + [pad]
    return jnp.concatenate(parts, axis=-1)


def _gates_kernel(x_ref, w_ref, alog_ref, dt_ref, gt_ref, gr_ref, *, hg, chunk):
    tm = x_ref.shape[0]
    pw = DN_PACK * chunk
    val = _dot(x_ref[...], w_ref[...])
    quantity = (lax.broadcasted_iota(jnp.int32, val.shape, 1) % V7X_LANES) // hg
    beta = jax.nn.sigmoid(val)
    z = val + dt_ref[...]
    softplus = jnp.maximum(z, 0.0) + jnp.log1p(jnp.exp(-jnp.abs(z)))
    g = jnp.where((quantity >= 1) & (quantity < GATE_GROUPS), -jnp.exp(alog_ref[...]) * softplus, 0.0)
    r = lax.broadcasted_iota(jnp.int32, (tm, tm), 0)
    c = lax.broadcasted_iota(jnp.int32, (tm, tm), 1)
    tri = (((r // chunk) == (c // chunk)) & (c <= r)).astype(BF16)
    hi = g.astype(BF16)
    rem = g - hi.astype(F32)
    mid = rem.astype(BF16)
    lo = (rem - mid.astype(F32)).astype(BF16)
    gc = _dot(tri, hi) + _dot(tri, mid) + _dot(tri, lo)
    total = jnp.concatenate(
        [jnp.broadcast_to(gc[(ci + 1) * chunk - 1:(ci + 1) * chunk, :], (chunk, gc.shape[1]))
         for ci in range(tm // chunk)], axis=0)
    gt_ref[...] = jnp.where(quantity == 0, beta,
                  jnp.where(quantity == 1, gc,
                  jnp.where(quantity == 2, jnp.exp(gc),
                  jnp.where(quantity == 3, jnp.exp(total - gc),
                  jnp.where(quantity == 4, jnp.exp(total), 0.0)))))
    gct = gc.T
    for pi in range(tm // pw):
        for grp in range(gc.shape[1] // V7X_LANES):
            lo_row = grp * V7X_LANES + hg
            gr_ref[pi, grp * hg:(grp + 1) * hg, :] = gct[lo_row:lo_row + hg, pi * pw:(pi + 1) * pw]


def dn_gates(xn, w_gate, layer, a_log, dt_bias, *, hg):
    t, d = xn.shape
    n_heads = a_log.shape[0]
    gl = w_gate.shape[2]
    pw = DN_PACK * DN_CHUNK
    tm = _tile(t, 512, align=pw)
    alog = _gate_lanes(jnp.zeros_like(a_log), a_log, hg).reshape(1, gl)
    dt = _gate_lanes(jnp.zeros_like(dt_bias), dt_bias, hg).reshape(1, gl)
    row = pl.BlockSpec((1, gl), lambda i: (0, 0))
    kern = functools.partial(_gates_kernel, hg=hg, chunk=DN_CHUNK)
    return pl.pallas_call(
        kern,
        grid=(t // tm,),
        in_specs=[pl.BlockSpec((tm, d), lambda i: (i, 0)),
                  pl.BlockSpec((None, d, gl), lambda i: (layer, 0, 0)), row, row],
        out_specs=[pl.BlockSpec((tm, gl), lambda i: (i, 0)),
                   pl.BlockSpec((tm // pw, n_heads, pw), lambda i: (i, 0, 0))],
        out_shape=[jax.ShapeDtypeStruct((t, gl), F32),
                   jax.ShapeDtypeStruct((t // pw, n_heads, pw), F32)],
        compiler_params=_params(("parallel",), tm * d * 2 + d * gl * 2 + 4 * tm * gl * 4 + 4 * tm * tm),
        name="dn_gates",
    )(xn, w_gate, alog, dt)


def _block_diag(xp):
    n, m = xp.shape
    tiled = jnp.concatenate([xp] * (m // n), axis=0)
    r = lax.broadcasted_iota(jnp.int32, (m, m), 0) // n
    c = lax.broadcasted_iota(jnp.int32, (m, m), 1) // n
    return jnp.where(r == c, tiled, jnp.zeros_like(tiled))


def _packed_dot(xp, yp, passes):
    xh = xp.astype(BF16)
    yh = yp.astype(BF16)
    out = _dot(xh, _block_diag(yh))
    if passes >= 3:
        xl = (xp - xh.astype(F32)).astype(BF16)
        yl = (yp - yh.astype(F32)).astype(BF16)
        out = out + _dot(xh, _block_diag(yl)) + _dot(xl, _block_diag(yh))
    return out


def _packed_unit_lower_inverses(aps, passes):
    n = aps[0].shape[0]
    r = lax.broadcasted_iota(jnp.int32, aps[0].shape, 0)
    c = lax.broadcasted_iota(jnp.int32, aps[0].shape, 1) % n
    eye = (r == c).astype(F32)
    size = INV_BASE
    in_blk = (r // size) == (c // size)
    xs = [jnp.where(in_blk, -ap, 0.0) for ap in aps]
    invs = [eye + x for x in xs]
    power = 1
    while 2 * power < size:
        xs = [_packed_dot(x, x, passes) for x in xs]
        invs = [inv + _packed_dot(inv, x, passes) for inv, x in zip(invs, xs)]
        power *= 2
    while size < n:
        merged = (r // (2 * size)) == (c // (2 * size))
        offs = [_packed_dot(jnp.where(merged & ~in_blk, ap, 0.0), inv, passes) for ap, inv in zip(aps, invs)]
        invs = [inv - _packed_dot(inv, off, passes) for inv, off in zip(invs, offs)]
        in_blk = merged
        size *= 2
    return invs


def _dn_kernel(q_ref, k_ref, v_ref, z_ref, wq_ref, wk_ref, wv_ref, gt_ref, gr_ref, nw_ref, o_ref,
               qtail, ktail, vtail, qn, kn, vn, rhs_s, u_s, wq_s, kd_s, qkm_s, vnew_s, qs_s, state,
               *, ts, hg, chunk):
    s = pl.program_id(2)
    gi = pl.program_id(1)
    hd = HEAD_DIM
    halo = V7X_SUBLANES

    @pl.when(s == 0)
    def _():
        state[...] = jnp.zeros(state.shape, F32)
        for tail in (qtail, ktail, vtail):
            tail[...] = jnp.zeros(tail.shape, F32)

    def conv_silu(src_ref, tail, w_ref):
        x = src_ref[...].astype(F32)
        xe = jnp.concatenate([tail[...], x], axis=0)
        tail[...] = x[ts - halo:, :]
        return _silu(_causal_conv(xe, w_ref[...])[halo:])

    def l2norm(x, scale):
        return x * (lax.rsqrt(jnp.sum(x * x, axis=-1, keepdims=True) + EPS) * scale)

    qc = conv_silu(q_ref, qtail, wq_ref)
    kc = conv_silu(k_ref, ktail, wk_ref)
    vn[...] = conv_silu(v_ref, vtail, wv_ref)
    for hh in range(hg):
        cols = slice(hh * hd, (hh + 1) * hd)
        qn[:, cols] = l2norm(qc[:, cols], hd ** -0.5)
        kn[:, cols] = l2norm(kc[:, cols], 1.0)

    pack = DN_PACK
    pw = pack * chunk
    n_pack = ts // pw
    r = lax.broadcasted_iota(jnp.int32, (pw, pw), 0)
    c = lax.broadcasted_iota(jnp.int32, (pw, pw), 1)
    same = (r // chunk) == (c // chunk)
    tri = same & (r >= c)
    strict = same & (r > c)
    lane = lax.broadcasted_iota(jnp.int32, (pw, V7X_LANES), 1)
    nw = nw_ref[...]

    jobs = [(hh, p) for hh in range(hg) for p in range(n_pack)]
    chunk_decay = {}
    aps = []
    for hh, p in jobs:
        cols = slice(hh * hd, (hh + 1) * hd)
        rows = slice(p * pw, (p + 1) * pw)
        gt = gt_ref[rows, :]
        beta, gc, eg, erem, etot = (jnp.sum(jnp.where(lane == qi * hg + hh, gt, 0.0), axis=-1, keepdims=True)
                                    for qi in range(GATE_GROUPS))
        gc_row = gr_ref[p, pl.ds(gi * hg + hh, 1), :]
        q = qn[rows, cols]
        k = kn[rows, cols]
        v = vn[rows, cols]
        kb = k.astype(BF16)
        sims = _dot_nt(jnp.concatenate([kb, q.astype(BF16)], axis=0), kb)
        decay = jnp.exp(jnp.where(tri, gc - gc_row, -jnp.inf))
        a_full = jnp.where(strict, beta * sims[0:pw] * decay, 0.0)
        ap = a_full[0:chunk]
        for bb in range(1, pack):
            ap = ap + a_full[bb * chunk:(bb + 1) * chunk]
        aps.append(ap)
        rhs_s[hh, rows, :] = jnp.concatenate([v * beta, k * (beta * eg)], axis=1).astype(BF16)
        qg = (q * eg).astype(BF16)
        for bb in range(pack):
            wq_s[hh, p * pack + bb, chunk:2 * chunk, :] = qg[bb * chunk:(bb + 1) * chunk]
            chunk_decay[(hh, p * pack + bb)] = etot[bb * chunk:bb * chunk + 1, :]
        kd_s[hh, rows, :] = (k * erem).astype(BF16)
        qkm_s[hh, p] = (sims[pw:2 * pw] * decay).astype(BF16)

    tps = _packed_unit_lower_inverses(aps, INV_PASSES)

    for (hh, p), tp in zip(jobs, tps):
        rows = slice(p * pw, (p + 1) * pw)
        uw = _dot(_block_diag(tp.astype(BF16)), rhs_s[hh, rows, :])
        u_s[hh, rows, :] = uw[:, 0:hd]
        w = uw[:, hd:2 * hd].astype(BF16)
        for bb in range(pack):
            wq_s[hh, p * pack + bb, 0:chunk, :] = w[bb * chunk:(bb + 1) * chunk]

    for cidx in range(ts // chunk):
        rows = slice(cidx * chunk, (cidx + 1) * chunk)
        for hh in range(hg):
            st = state[hh]
            ws_qs = _dot(wq_s[hh, cidx], st.astype(BF16))
            v_new = (u_s[hh, rows, :] - ws_qs[0:chunk]).astype(BF16)
            vnew_s[hh, rows, :] = v_new
            qs_s[hh, rows, :] = ws_qs[chunk:2 * chunk]
            state[hh] = st * chunk_decay[(hh, cidx)] + _dot_tn(kd_s[hh, rows, :], v_new)

    for hh in range(hg):
        cols = slice(hh * hd, (hh + 1) * hd)
        for p in range(n_pack):
            rows = slice(p * pw, (p + 1) * pw)
            o = qs_s[hh, rows, :] + _dot(qkm_s[hh, p], vnew_s[hh, rows, :])
            zz = z_ref[rows, cols].astype(F32)
            o = o * lax.rsqrt(jnp.mean(o * o, axis=-1, keepdims=True) + EPS) * nw * _silu(zz)
            o_ref[rows, cols] = o.astype(o_ref.dtype)


def delta_net(proj, proj_z, gt, gr, conv_w, layer, norm_w, *, seq, q_off, k_off, v_off, z_off, n_heads):
    b = proj.shape[0]
    hd = HEAD_DIM
    hg = min(DN_HEADS_PER_STEP, n_heads)
    pw = DN_PACK * DN_CHUNK
    ts = _tile(seq, 512, align=pw)
    gw = hg * hd
    kw = n_heads * hd
    assert n_heads % hg == 0
    for off in (q_off, k_off, v_off, z_off, kw):
        assert off % gw == 0
    kern = functools.partial(_dn_kernel, ts=ts, hg=hg, chunk=DN_CHUNK)

    def col_spec(off):
        return pl.BlockSpec((None, ts, gw), lambda bi, g, s: (bi, s, off // gw + g))

    def w_spec(off):
        return pl.BlockSpec((None, DN_CONV, gw), lambda bi, g, s: (layer, 0, off // gw + g))

    scratch = ([pltpu.VMEM((V7X_SUBLANES, gw), F32)] * 3
               + [pltpu.VMEM((ts, gw), F32)] * 3
               + [pltpu.VMEM((hg, ts, 2 * hd), BF16),
                  pltpu.VMEM((hg, ts, hd), F32),
                  pltpu.VMEM((hg, ts // DN_CHUNK, 2 * DN_CHUNK, hd), BF16),
                  pltpu.VMEM((hg, ts, hd), BF16),
                  pltpu.VMEM((hg, ts // pw, pw, pw), BF16),
                  pltpu.VMEM((hg, ts, hd), BF16),
                  pltpu.VMEM((hg, ts, hd), F32),
                  pltpu.VMEM((hg, hd, hd), F32)])
    block_bytes = (5 * ts * gw * 2 + ts * V7X_LANES * 4 + (ts // pw) * n_heads * pw * 4
                   + 3 * (ts + V7X_SUBLANES) * gw * 4 + 3 * ts * gw * 4 + hg * ts * hd * 18)
    return pl.pallas_call(
        kern,
        grid=(b, n_heads // hg, seq // ts),
        in_specs=[col_spec(q_off), col_spec(k_off), col_spec(v_off), col_spec(z_off),
                  w_spec(0), w_spec(kw), w_spec(2 * kw),
                  pl.BlockSpec((None, ts, V7X_LANES), lambda bi, g, s: (bi, s, g)),
                  pl.BlockSpec((None, ts // pw, n_heads, pw), lambda bi, g, s: (bi, s, 0, 0)),
                  pl.BlockSpec((1, hd), lambda bi, g, s: (0, 0))],
        out_specs=pl.BlockSpec((None, ts, gw), lambda bi, g, s: (bi, s, g)),
        out_shape=jax.ShapeDtypeStruct((b, seq, kw), BF16),
        scratch_shapes=scratch,
        compiler_params=_params(("parallel", "parallel", "arbitrary"), block_bytes),
        name="delta_net",
    )(proj, proj, proj, proj_z, conv_w, conv_w, conv_w, gt, gr, norm_w.reshape(1, hd))


def _merge_kernel(oa_ref, od_ref, ga_ref, gd_ref, wa_ref, wd_ref, y_ref):
    a = _dot(oa_ref[...], wa_ref[...])
    sa = jax.nn.sigmoid(ga_ref[...].astype(F32))
    sd = jax.nn.sigmoid(gd_ref[...].astype(F32))
    d = _dot(od_ref[...], wd_ref[...])
    y_ref[...] = (sa * a + sd * d).astype(y_ref.dtype)


def branch_merge(o_a, o_d, proj, w_a, w_d, layer, *, ga_off, gd_off):
    t, ka = o_a.shape
    kd = o_d.shape[1]
    n = w_a.shape[2]
    tm = _tile(t, 1024)
    tn = _tile(n, 1024)
    assert ga_off % tn == 0 and gd_off % tn == 0
    block_bytes = 2 * (tm * ka + tm * kd + 3 * tm * tn + ka * tn + kd * tn)
    return pl.pallas_call(
        _merge_kernel,
        grid=(t // tm, n // tn),
        in_specs=[pl.BlockSpec((tm, ka), lambda i, j: (i, 0)),
                  pl.BlockSpec((tm, kd), lambda i, j: (i, 0)),
                  pl.BlockSpec((tm, tn), lambda i, j: (i, ga_off // tn + j)),
                  pl.BlockSpec((tm, tn), lambda i, j: (i, gd_off // tn + j)),
                  pl.BlockSpec((None, ka, tn), lambda i, j: (layer, 0, j)),
                  pl.BlockSpec((None, kd, tn), lambda i, j: (layer, 0, j))],
        out_specs=pl.BlockSpec((tm, tn), lambda i, j: (i, j)),
        out_shape=jax.ShapeDtypeStruct((t, n), BF16),
        compiler_params=_params(("parallel", "parallel"), block_bytes),
        name="branch_merge",
    )(o_a, o_d, proj, proj, w_a, w_d)


def _proj_norm_res_kernel(*refs, with_next):
    if with_next:
        x_ref, w_ref, h_ref, post_ref, next_ref, h_out, xn_out = refs
    else:
        x_ref, w_ref, h_ref, post_ref, h_out = refs
    hn = h_ref[...] + _rms(_dot(x_ref[...], w_ref[...]), post_ref[...])
    h_out[...] = hn
    if with_next:
        xn_out[...] = _rms(hn, next_ref[...]).astype(xn_out.dtype)


def proj_norm_residual(x, w, layer, h, post_w, next_w):
    t, k = x.shape
    d = w.shape[2]
    tm = _tile(t, 256)
    with_next = next_w is not None
    kern = functools.partial(_proj_norm_res_kernel, with_next=with_next)
    row = pl.BlockSpec((1, d), lambda i: (0, 0))
    tile = pl.BlockSpec((tm, d), lambda i: (i, 0))
    in_specs = [pl.BlockSpec((tm, k), lambda i: (i, 0)),
                pl.BlockSpec((None, k, d), lambda i: (layer, 0, 0), pipeline_mode=pl.Buffered(1)), tile, row]
    args = [x, w, h, post_w.reshape(1, d)]
    out_specs = [tile]
    out_shape = [jax.ShapeDtypeStruct((t, d), F32)]
    if with_next:
        in_specs.append(row)
        args.append(next_w.reshape(1, d))
        out_specs.append(tile)
        out_shape.append(jax.ShapeDtypeStruct((t, d), BF16))
    block_bytes = tm * k * 2 + tm * d * (4 + 4 + 2) + (k * d * 2) // 2 + tm * d * 4
    res = pl.pallas_call(
        kern,
        grid=(t // tm,),
        in_specs=in_specs,
        out_specs=out_specs,
        out_shape=out_shape,
        compiler_params=pltpu.CompilerParams(
            dimension_semantics=("parallel",),
            vmem_limit_bytes=int(min(V7X_VMEM_BYTES - (2 << 20), 2 * block_bytes + (4 << 20)))),
        name="proj_norm_residual",
    )(*args)
    return (res[0], res[1]) if with_next else (res[0], None)


def _causal_conv(u, taps):
    y = u * taps[0:1, :]
    for j in range(1, taps.shape[0]):
        y = pltpu.roll(y, 1, 0) + u * taps[j:j + 1, :]
    return y


def _ffn_up_kernel(x_ref, wg_ref, wv_ref, cg_ref, cv_ref, bg_ref, bv_ref, o_ref, tail_g, tail_v,
                   *, tiles_per_seq, sub):
    i = pl.program_id(1)
    halo = V7X_SUBLANES
    tm, tn = o_ref.shape
    x = x_ref[...]
    blocks = [slice(lo, min(lo + sub, tn)) for lo in range(0, tn, sub)]

    @pl.when(i == 0)
    def _():
        tail_g[...] = jnp.zeros(tail_g.shape, F32)
        tail_v[...] = jnp.zeros(tail_v.shape, F32)

    def project(cols):
        return _dot(x, wg_ref[:, cols]), _dot(x, wv_ref[:, cols])

    def conv(u, tail, cols, c_ref, b_ref):
        hist = jnp.where((i % tiles_per_seq) != 0, tail[:, cols], 0.0)
        tail[:, cols] = u[tm - halo:, :]
        return _causal_conv(jnp.concatenate([hist, u], axis=0), c_ref[:, cols])[halo:] + b_ref[:, cols]

    pending = project(blocks[0])
    for s, cols in enumerate(blocks):
        ug, uv = pending
        if s + 1 < len(blocks):
            pending = project(blocks[s + 1])
        gate = conv(ug, tail_g, cols, cg_ref, bg_ref)
        val = conv(uv, tail_v, cols, cv_ref, bv_ref)
        o_ref[:, cols] = (_silu(gate) * val).astype(o_ref.dtype)


def ffn_up(xn, w_up, conv_w, conv_b, layer, *, seq):
    t, d = xn.shape
    dff = w_up.shape[2] // 2
    tm = _tile(seq, 1024)
    sub = 8 * V7X_LANES
    tn = _tile(dff, 2816)
    halo = V7X_SUBLANES
    nj = dff // tn
    kern = functools.partial(_ffn_up_kernel, tiles_per_seq=seq // tm, sub=sub)

    def w_spec(off):
        return pl.BlockSpec((None, d, tn), lambda j, i: (layer, 0, off + j), pipeline_mode=pl.Buffered(1))

    block_bytes = tm * d * 2 + tm * tn * 2 + d * tn * 2 + 2 * (tm + halo) * sub * 4
    return pl.pallas_call(
        kern,
        grid=(nj, t // tm),
        in_specs=[pl.BlockSpec((tm, d), lambda j, i: (i, 0)),
                  w_spec(0), w_spec(nj),
                  pl.BlockSpec((None, FFN_CONV, tn), lambda j, i: (layer, 0, j)),
                  pl.BlockSpec((None, FFN_CONV, tn), lambda j, i: (layer, 0, nj + j)),
                  pl.BlockSpec((None, 1, tn), lambda j, i: (layer, 0, j)),
                  pl.BlockSpec((None, 1, tn), lambda j, i: (layer, 0, nj + j))],
        out_specs=pl.BlockSpec((tm, tn), lambda j, i: (i, j)),
        out_shape=jax.ShapeDtypeStruct((t, dff), BF16),
        scratch_shapes=[pltpu.VMEM((halo, tn), F32)] * 2,
        compiler_params=_params(("arbitrary", "arbitrary"), block_bytes),
        name="ffn_up",
    )(xn, w_up, w_up, conv_w, conv_w, conv_b, conv_b)


def kernel(x, positions, norm_mix_pre, w_in, attn_sinks, dn_conv_w, dn_a_log, dn_dt_bias, dn_norm_w,
           w_branch_attn, w_branch_dn, w_out, norm_mix_post, norm_ffn_pre, w_ffn_up, ffn_conv_w,
           ffn_conv_b, w_ffn_down, norm_ffn_post):
    b, seq, d = x.shape
    depth = w_in.shape[0]
    t = b * seq
    n_attn = attn_sinks.shape[1]
    n_dn = dn_a_log.shape[1]
    qa_w = n_attn * HEAD_DIM
    kv_w = ATTN_KV_HEADS * HEAD_DIM
    dn_w = n_dn * HEAD_DIM

    edges = [0]
    for wdt in (qa_w, kv_w, kv_w, 3 * dn_w, n_dn, n_dn, dn_w, d, d):
        edges.append(edges[-1] + wdt)
    head_w = edges[4]
    tail_w = edges[9] - edges[6]
    w_head = w_in[:, :, :head_w].astype(BF16)
    w_tail = w_in[:, :, edges[6]:edges[9]].astype(BF16)
    dn_hg = min(DN_HEADS_PER_STEP, n_dn)
    w_gate = _gate_lanes(w_in[:, :, edges[4]:edges[5]], w_in[:, :, edges[5]:edges[6]], dn_hg).astype(BF16)
    q_off, k_off, v_off = 0, qa_w, qa_w + kv_w
    dq_off = qa_w + 2 * kv_w
    dk_off, dv_off = dq_off + dn_w, dq_off + 2 * dn_w
    z_off, ga_off, gd_off = 0, dn_w, dn_w + d

    w_ba = w_branch_attn.astype(BF16)
    w_bd = w_branch_dn.astype(BF16)
    w_o = w_out.astype(BF16)
    w_up = w_ffn_up.astype(BF16)
    w_dn = w_ffn_down.astype(BF16)
    ffn_b = ffn_conv_b.reshape(depth, 1, ffn_conv_b.shape[1])

    cos, sin = rope_tables(positions)
    cos = cos.reshape(b, seq, HEAD_DIM)
    sin = sin.reshape(b, seq, HEAD_DIM)

    h = x.reshape(t, d)
    xn = rmsnorm(h, norm_mix_pre[0])
    for l in range(depth):
        proj_head = matmul(xn, w_head, l, head_w)
        proj_tail = matmul(xn, w_tail, l, tail_w)
        head3 = proj_head.reshape(b, seq, head_w)
        o_a = attention(head3, cos, sin, attn_sinks[l], seq=seq, q_off=q_off, k_off=k_off,
                        v_off=v_off, n_heads=n_attn)
        gt, gr = dn_gates(xn, w_gate, l, dn_a_log[l], dn_dt_bias[l], hg=dn_hg)
        o_d = delta_net(head3, proj_tail.reshape(b, seq, tail_w), gt.reshape(b, seq, gt.shape[1]),
                        gr.reshape(b, seq // (DN_PACK * DN_CHUNK), n_dn, DN_PACK * DN_CHUNK),
                        dn_conv_w, l, dn_norm_w[l],
                        seq=seq, q_off=dq_off, k_off=dk_off, v_off=dv_off, z_off=z_off, n_heads=n_dn)
        y = branch_merge(o_a.reshape(t, qa_w), o_d.reshape(t, dn_w), proj_tail, w_ba, w_bd, l,
                         ga_off=ga_off, gd_off=gd_off)
        h, xf = proj_norm_residual(y, w_o, l, h, norm_mix_post[l], norm_ffn_pre[l])
        u = ffn_up(xf, w_up, ffn_conv_w, ffn_b, l, seq=seq)
        next_w = norm_mix_pre[l + 1] if l + 1 < depth else None
        h, xn = proj_norm_residual(u, w_dn, l, h, norm_ffn_post[l], next_w)
    return h.reshape(b, seq, d)
```
